```python
import jax
import jax.numpy as jnp
from jax import lax
import numpy as np

D_MODEL = 1024
BATCH = 2
SEQ = 8192
DEPTH = 2

GRID_W = 64
CTX_LEN = 256
N_GROUPS = 4
GROUP_WIDTH = D_MODEL // N_GROUPS
HEAD_DIM = 64
BLOCK = 128
ROPE_THETA = 10000.0
EPS = 1e-6

A_HEADS = GROUP_WIDTH // HEAD_DIM
A_KV_HEADS = 2
A_GROUP = A_HEADS // A_KV_HEADS
WINDOW = 128
B_HEADS = GROUP_WIDTH // HEAD_DIM
B_KV_HEADS = 2
B_GROUP = B_HEADS // B_KV_HEADS
C_HEADS = GROUP_WIDTH // HEAD_DIM
NA_KH = 8
NA_KW = 16
D_HEADS = 4
MLA_V = GROUP_WIDTH // D_HEADS
MLA_NOPE = 64
MLA_ROPE = 32
MLA_Q_RANK = 192
MLA_KV_RANK = 128
D_FF = 2816
N_EXPERTS = 8
TOP_K = 2
D_FF_EXPERT = 3584
MOE_BLOCK = 512
N_DENSE = (DEPTH + 1) // 2
N_MOE = DEPTH // 2

IN_SIZES = (A_HEADS * HEAD_DIM, A_KV_HEADS * HEAD_DIM, A_KV_HEADS * HEAD_DIM,
            B_HEADS * HEAD_DIM, B_KV_HEADS * HEAD_DIM, B_KV_HEADS * HEAD_DIM,
            C_HEADS * HEAD_DIM, C_HEADS * HEAD_DIM, C_HEADS * HEAD_DIM,
            MLA_Q_RANK, MLA_KV_RANK, MLA_ROPE)
IN_WIDTH = sum(IN_SIZES)
SCALE_HEAD = HEAD_DIM ** -0.5
SCALE_MLA = (MLA_NOPE + MLA_ROPE) ** -0.5

kernel_name = 'hybrid_parallel_group_dit_block'


def rms_norm(x, g):
    xf = x.astype(jnp.float32)
    y = xf * lax.rsqrt(jnp.mean(xf * xf, axis=-1, keepdims=True) + EPS)
    return (y * g.astype(jnp.float32)).astype(x.dtype)


def modulate(h, shift, scale):
    return h * (1 + scale) + shift


def axial_rope_tables(n_tokens, rot_dim, dtype):
    t = jnp.arange(n_tokens)
    row = (t // GRID_W).astype(jnp.float32)[:, None]
    col = (t % GRID_W).astype(jnp.float32)[:, None]
    half = rot_dim // 2
    inv_freq = ROPE_THETA ** (-jnp.arange(0, half, 2, dtype=jnp.float32) / half)
    ang = jnp.concatenate([row * inv_freq, row * inv_freq, col * inv_freq, col * inv_freq], axis=-1)
    return jnp.cos(ang).astype(dtype), jnp.sin(ang).astype(dtype)


def apply_rope(t, cos, sin):
    r1, r2, c1, c2 = jnp.split(t, 4, axis=-1)
    rot = jnp.concatenate([-r2, r1, -c2, c1], axis=-1)
    return t * cos + rot * sin


def rope_full(t, rope):
    return t if rope is None else apply_rope(t, *rope)


def rope_tail(t, rope):
    if rope is None:
        return t
    return jnp.concatenate([t[..., :MLA_NOPE], apply_rope(t[..., MLA_NOPE:], *rope)], axis=-1)


def split_heads_q(t, n_kv, group):
    b, n, _ = t.shape
    return t.reshape(b, n, n_kv, group, -1).transpose(0, 2, 3, 1, 4)


def split_heads_kv(t, n_kv):
    b, n, _ = t.shape
    return t.reshape(b, n, n_kv, -1).transpose(0, 2, 1, 3)


def merge_heads(o):
    b, hk, g, n, d = o.shape
    return o.transpose(0, 3, 1, 2, 4).reshape(b, n, hk * g * d)


def softmax_attend(q, k, v, scale, sink=None):
    s = jnp.einsum('bhgqd,bhkd->bhgqk', q, k).astype(jnp.float32) * scale
    if sink is None:
        p = jax.nn.softmax(s, axis=-1)
    else:
        sk = jnp.broadcast_to(sink.astype(jnp.float32)[None, :, :, None, None], s.shape[:-1] + (1,))
        p = jax.nn.softmax(jnp.concatenate([s, sk], axis=-1), axis=-1)[..., :-1]
    return jnp.einsum('bhgqk,bhkd->bhgqd', p.astype(v.dtype), v)


def blockwise_dense(q, k, v, scale):
    b, hk, g, s, d = q.shape
    nb = s // BLOCK
    qb = q.reshape(b, hk, g, nb, BLOCK, d).transpose(3, 0, 1, 2, 4, 5)
    ob = lax.map(lambda qi: softmax_attend(qi, k, v, scale), qb)
    return ob.transpose(1, 2, 3, 0, 4, 5).reshape(b, hk, g, s, -1)


def window_attn_latent(q, k, v, kc, vc, sink):
    b, hk, g, s, d = q.shape
    nb = s // BLOCK
    qb = q.reshape(b, hk, g, nb, BLOCK, d)

    def band(t):
        tp = jnp.pad(t, ((0, 0), (0, 0), (BLOCK, BLOCK), (0, 0)))
        tb = tp.reshape(b, hk, nb + 2, BLOCK, t.shape[-1])
        return jnp.concatenate([tb[:, :, :-2], tb[:, :, 1:-1], tb[:, :, 2:]], axis=3)

    kb, vb = band(k), band(v)
    s_loc = jnp.einsum('bhgnqd,bhnkd->bhgnqk', qb, kb).astype(jnp.float32) * SCALE_HEAD
    qpos = jnp.arange(nb)[:, None, None] * BLOCK + jnp.arange(BLOCK)[None, :, None]
    kpos = jnp.arange(nb)[:, None, None] * BLOCK - BLOCK + jnp.arange(3 * BLOCK)[None, None, :]
    valid = (jnp.abs(kpos - qpos) <= WINDOW) & (kpos >= 0) & (kpos < s)
    s_loc = jnp.where(valid, s_loc, -jnp.inf)
    s_ctx = jnp.einsum('bhgnqd,bhcd->bhgnqc', qb, kc).astype(jnp.float32) * SCALE_HEAD
    s_sink = jnp.broadcast_to(sink.astype(jnp.float32)[None, :, :, None, None, None], s_ctx.shape[:-1] + (1,))
    p = jax.nn.softmax(jnp.concatenate([s_loc, s_ctx, s_sink], axis=-1), axis=-1)
    n_loc = 3 * BLOCK
    p_loc = p[..., :n_loc].astype(v.dtype)
    p_ctx = p[..., n_loc:n_loc + kc.shape[2]].astype(v.dtype)
    o = jnp.einsum('bhgnqk,bhnkd->bhgnqd', p_loc, vb) + jnp.einsum('bhgnqc,bhcd->bhgnqd', p_ctx, vc)
    return o.reshape(b, hk, g, s, d)


def neighbourhood_attn_latent(q, k, v, kc, vc, rpb):
    b, h, s, d = q.shape
    rows = s // GRID_W
    kh = min(NA_KH, rows)
    kw = NA_KW
    qg = q.reshape(b, h, rows, GRID_W, d)
    kg = k.reshape(b, h, rows, GRID_W, d)
    vg = v.reshape(b, h, rows, GRID_W, d)
    j = jnp.arange(GRID_W)
    col_idx = jnp.clip(j - kw // 2, 0, GRID_W - kw)[:, None] + jnp.arange(kw)[None, :]
    dc = col_idx - j[:, None] + NA_KW - 1

    def row_step(r):
        rs = jnp.clip(r - kh // 2, 0, rows - kh)
        q_r = lax.dynamic_index_in_dim(qg, r, axis=2, keepdims=False)
        k_win = lax.dynamic_slice_in_dim(kg, rs, kh, axis=2)[:, :, :, col_idx]
        v_win = lax.dynamic_slice_in_dim(vg, rs, kh, axis=2)[:, :, :, col_idx]
        dr = rs + jnp.arange(kh) - r + NA_KH - 1
        bias = rpb[:, dr[:, None, None], dc[None, :, :]].transpose(0, 2, 1, 3)
        s_win = (jnp.einsum('bhjd,bhrjcd->bhjrc', q_r, k_win).astype(jnp.float32) * SCALE_HEAD
                 + bias[None].astype(jnp.float32)).reshape(b, h, GRID_W, kh * kw)
        s_ctx = jnp.einsum('bhjd,bhcd->bhjc', q_r, kc).astype(jnp.float32) * SCALE_HEAD
        p = jax.nn.softmax(jnp.concatenate([s_win, s_ctx], axis=-1), axis=-1)
        p_win = p[..., :kh * kw].reshape(b, h, GRID_W, kh, kw).astype(v.dtype)
        p_ctx = p[..., kh * kw:].astype(v.dtype)
        return (jnp.einsum('bhjrc,bhrjcd->bhjd', p_win, v_win)
                + jnp.einsum('bhjc,bhcd->bhjd', p_ctx, vc))

    o = lax.map(row_step, jnp.arange(rows))
    return o.transpose(1, 2, 0, 3, 4).reshape(b, h, 1, s, d)


def project_streams(h, w_in, a_qn, a_kn, b_qn, b_kn, c_qn, c_kn, d_q_norm, d_w_uq, d_kv_norm, d_w_ukv,
                    d_qn, d_kn, rope64, rope32, with_q):
    offs = np.cumsum(IN_SIZES)[:-1].tolist()
    a_q, a_k, a_v, b_q, b_k, b_v, c_q, c_k, c_v, d_cq, d_ckv, d_kr = jnp.split(h @ w_in, offs, axis=-1)
    bsz, n, _ = h.shape
    out = {}
    out['a_k'] = rope_full(rms_norm(split_heads_kv(a_k, A_KV_HEADS), a_kn), rope64)
    out['a_v'] = split_heads_kv(a_v, A_KV_HEADS)
    out['b_k'] = rope_full(rms_norm(split_heads_kv(b_k, B_KV_HEADS), b_kn), rope64)
    out['b_v'] = split_heads_kv(b_v, B_KV_HEADS)
    out['c_k'] = rms_norm(split_heads_kv(c_k, C_HEADS), c_kn)
    out['c_v'] = split_heads_kv(c_v, C_HEADS)
    kv = (rms_norm(d_ckv, d_kv_norm) @ d_w_ukv).reshape(bsz, n, D_HEADS, MLA_NOPE + MLA_V)
    kr = jnp.broadcast_to(d_kr[:, :, None, :], (bsz, n, D_HEADS, MLA_ROPE))
    k_mla = rms_norm(jnp.concatenate([kv[..., :MLA_NOPE], kr], axis=-1), d_kn).transpose(0, 2, 1, 3)
    out['d_k'] = rope_tail(k_mla, rope32)
    out['d_v'] = kv[..., MLA_NOPE:].transpose(0, 2, 1, 3)
    if with_q:
        out['a_q'] = rope_full(rms_norm(split_heads_q(a_q, A_KV_HEADS, A_GROUP), a_qn), rope64)
        out['b_q'] = rope_full(rms_norm(split_heads_q(b_q, B_KV_HEADS, B_GROUP), b_qn), rope64)
        out['c_q'] = rms_norm(split_heads_kv(c_q, C_HEADS), c_qn)
        q_mla = (rms_norm(d_cq, d_q_norm) @ d_w_uq).reshape(bsz, n, D_HEADS, MLA_NOPE + MLA_ROPE)
        q_mla = rms_norm(q_mla, d_qn).transpose(0, 2, 1, 3)[:, :, None]
        out['d_q'] = rope_tail(q_mla, rope32)
    return out


def merge_groups(o_a, o_b, o_c, o_d, g, w_out):
    y = jnp.stack([merge_heads(o_a), merge_heads(o_b), merge_heads(o_c), merge_heads(o_d)], axis=2)
    y = rms_norm(y, g.reshape(N_GROUPS, GROUP_WIDTH))
    return y.reshape(y.shape[0], y.shape[1], D_MODEL) @ w_out


def token_mixers(n_lat, n_ctx, w_in, a_qn, a_kn, a_sink, b_qn, b_kn, c_qn, c_kn, c_rpb, d_q_norm, d_w_uq,
                 d_kv_norm, d_w_ukv, d_qn, d_kn, out_norm_g, w_out, rope64, rope32, with_ctx_out):
    proj = (a_qn, a_kn, b_qn, b_kn, c_qn, c_kn, d_q_norm, d_w_uq, d_kv_norm, d_w_ukv, d_qn, d_kn)
    lat = project_streams(n_lat, w_in, *proj, rope64, rope32, True)
    cx = project_streams(n_ctx, w_in, *proj, None, None, with_ctx_out)
    sink = a_sink.reshape(A_KV_HEADS, A_GROUP)

    def cat(name):
        return jnp.concatenate([cx[name], lat[name]], axis=2)

    o_a = window_attn_latent(lat['a_q'], lat['a_k'], lat['a_v'], cx['a_k'], cx['a_v'], sink)
    o_b = blockwise_dense(lat['b_q'], cat('b_k'), cat('b_v'), SCALE_HEAD)
    o_c = neighbourhood_attn_latent(lat['c_q'], lat['c_k'], lat['c_v'], cx['c_k'], cx['c_v'], c_rpb)
    o_d = blockwise_dense(lat['d_q'], cat('d_k'), cat('d_v'), SCALE_MLA)
    out_lat = merge_groups(o_a, o_b, o_c, o_d, out_norm_g, w_out)
    if not with_ctx_out:
        return out_lat, None
    oc_a = softmax_attend(cx['a_q'], cx['a_k'], cx['a_v'], SCALE_HEAD, sink)
    oc_b = softmax_attend(cx['b_q'], cx['b_k'], cx['b_v'], SCALE_HEAD)
    oc_c = softmax_attend(cx['c_q'][:, :, None], cx['c_k'], cx['c_v'], SCALE_HEAD)
    oc_d = softmax_attend(cx['d_q'], cx['d_k'], cx['d_v'], SCALE_MLA)
    out_ctx = merge_groups(oc_a, oc_b, oc_c, oc_d, out_norm_g, w_out)
    return out_lat, out_ctx


def swiglu(h, w_gate, w_up, w_down):
    return (jax.nn.silu(h @ w_gate) * (h @ w_up)) @ w_down


def moe_swiglu(h, router, w_gate, w_up, w_down):
    bsz, n, d = h.shape
    hf = h.reshape(-1, d)
    n_tok = hf.shape[0]
    logits = (hf @ router).astype(jnp.float32)
    top_v, top_e = lax.top_k(logits, TOP_K)
    gates = jax.nn.softmax(top_v, axis=-1)
    m = n_tok * TOP_K
    e_flat = top_e.reshape(-1)
    tok_flat = jnp.repeat(jnp.arange(n_tok, dtype=jnp.int32), TOP_K)
    g_flat = gates.reshape(-1)
    order = jnp.argsort(e_flat)
    e_s, tok_s, g_s = e_flat[order], tok_flat[order], g_flat[order]
    counts = jnp.bincount(e_flat, length=N_EXPERTS)
    starts = jnp.cumsum(counts) - counts
    padded = (counts + MOE_BLOCK - 1) // MOE_BLOCK * MOE_BLOCK
    pends = jnp.cumsum(padded)
    pstarts = pends - padded
    dest = pstarts[e_s] + jnp.arange(m) - starts[e_s]
    n_blocks = (m + N_EXPERTS * (MOE_BLOCK - 1) + MOE_BLOCK - 1) // MOE_BLOCK
    n_rows = n_blocks * MOE_BLOCK
    buf_tok = jnp.zeros((n_rows,), jnp.int32).at[dest].set(tok_s)
    buf_gate = jnp.zeros((n_rows,), jnp.float32).at[dest].set(g_s)
    block_e = jnp.clip(jnp.searchsorted(pends, jnp.arange(n_blocks) * MOE_BLOCK, side='right'), 0, N_EXPERTS - 1)

    def run_block(args):
        e, toks = args
        xb = hf[toks]
        return (jax.nn.silu(xb @ w_gate[e]) * (xb @ w_up[e])) @ w_down[e]

    yb = lax.map(run_block, (block_e, buf_tok.reshape(n_blocks, MOE_BLOCK)))
    contrib = yb.reshape(n_rows, d) * buf_gate[:, None].astype(hf.dtype)
    return jnp.zeros_like(hf).at[buf_tok].add(contrib).reshape(bsz, n, d)


def setup_inputs(seed: int = 0) -> dict:
    key = jax.random.key(seed)
    ks = jax.random.split(key, 32)

    def nrm(i, shape, scale):
        return jax.random.normal(ks[i], shape, jnp.float32) * scale

    def gain(i, shape):
        return 1.0 + 0.05 * jax.random.normal(ks[i], shape, jnp.float32)

    L, d = DEPTH, D_MODEL
    return {
        'x': nrm(0, (BATCH, SEQ, d), 1.0),
        'c': nrm(1, (BATCH, d), 1.0),
        'ctx': nrm(2, (BATCH, CTX_LEN, d), 1.0),
        'c_ctx': nrm(3, (d,), 1.0),
        'ada_w': nrm(4, (L, d, 6 * d), 0.5 * d ** -0.5),
        'ada_b': nrm(5, (L, 6 * d), 0.02),
        'norm1_g': gain(6, (L, d)),
        'norm2_g': gain(7, (L, d)),
        'w_in': nrm(8, (L, d, IN_WIDTH), d ** -0.5),
        'a_qn': gain(9, (L, HEAD_DIM)),
        'a_kn': gain(10, (L, HEAD_DIM)),
        'a_sink': nrm(11, (L, A_HEADS), 0.5),
        'b_qn': gain(12, (L, HEAD_DIM)),
        'b_kn': gain(13, (L, HEAD_DIM)),
        'c_qn': gain(14, (L, HEAD_DIM)),
        'c_kn': gain(15, (L, HEAD_DIM)),
        'c_rpb': nrm(16, (L, C_HEADS, 2 * NA_KH - 1, 2 * NA_KW - 1), 0.1),
        'd_q_norm': gain(17, (L, MLA_Q_RANK)),
        'd_w_uq': nrm(18, (L, MLA_Q_RANK, D_HEADS * (MLA_NOPE + MLA_ROPE)), MLA_Q_RANK ** -0.5),
        'd_kv_norm': gain(19, (L, MLA_KV_RANK)),
        'd_w_ukv': nrm(20, (L, MLA_KV_RANK, D_HEADS * (MLA_NOPE + MLA_V)), MLA_KV_RANK ** -0.5),
        'd_qn': gain(21, (L, MLA_NOPE + MLA_ROPE)),
        'd_kn': gain(22, (L, MLA_NOPE + MLA_ROPE)),
        'out_norm_g': gain(23, (L, d)),
        'w_out': nrm(24, (L, d, d), d ** -0.5),
        'ffn_w_gate': nrm(25, (N_DENSE, d, D_FF), d ** -0.5),
        'ffn_w_up': nrm(26, (N_DENSE, d, D_FF), d ** -0.5),
        'ffn_w_down': nrm(27, (N_DENSE, D_FF, d), D_FF ** -0.5),
        'moe_router': nrm(28, (N_MOE, d, N_EXPERTS), d ** -0.5),
        'moe_w_gate': nrm(29, (N_MOE, N_EXPERTS, d, D_FF_EXPERT), d ** -0.5),
        'moe_w_up': nrm(30, (N_MOE, N_EXPERTS, d, D_FF_EXPERT), d ** -0.5),
        'moe_w_down': nrm(31, (N_MOE, N_EXPERTS, D_FF_EXPERT, d), D_FF_EXPERT ** -0.5),
    }


def reference(x, c, ctx, c_ctx, ada_w, ada_b, norm1_g, norm2_g, w_in, a_qn, a_kn, a_sink, b_qn, b_kn,
              c_qn, c_kn, c_rpb, d_q_norm, d_w_uq, d_kv_norm, d_w_ukv, d_qn, d_kn, out_norm_g, w_out,
              ffn_w_gate, ffn_w_up, ffn_w_down, moe_router, moe_w_gate, moe_w_up, moe_w_down):
    n_lat_tokens = x.shape[1]
    rope64 = axial_rope_tables(n_lat_tokens, HEAD_DIM, x.dtype)
    rope32 = axial_rope_tables(n_lat_tokens, MLA_ROPE, x.dtype)
    h_lat, h_ctx = x, ctx
    for layer in range(DEPTH):
        last = layer == DEPTH - 1
        mod_lat = jax.nn.silu(c) @ ada_w[layer] + ada_b[layer]
        mod_ctx = jax.nn.silu(c_ctx) @ ada_w[layer] + ada_b[layer]
        sh1, sc1, g1, sh2, sc2, g2 = jnp.split(mod_lat[:, None, :], 6, axis=-1)
        csh1, csc1, cg1, csh2, csc2, cg2 = jnp.split(mod_ctx, 6, axis=-1)
        mix_lat, mix_ctx = token_mixers(
            modulate(rms_norm(h_lat, norm1_g[layer]), sh1, sc1),
            modulate(rms_norm(h_ctx, norm1_g[layer]), csh1, csc1),
            w_in[layer], a_qn[layer], a_kn[layer], a_sink[layer], b_qn[layer], b_kn[layer],
            c_qn[layer], c_kn[layer], c_rpb[layer], d_q_norm[layer], d_w_uq[layer], d_kv_norm[layer],
            d_w_ukv[layer], d_qn[layer], d_kn[layer], out_norm_g[layer], w_out[layer],
            rope64, rope32, not last)
        h_lat = h_lat + g1 * mix_lat
        if not last:
            h_ctx = h_ctx + cg1 * mix_ctx
        f_lat = modulate(rms_norm(h_lat, norm2_g[layer]), sh2, sc2)
        j = layer // 2
        if layer % 2 == 0:
            h_lat = h_lat + g2 * swiglu(f_lat, ffn_w_gate[j], ffn_w_up[j], ffn_w_down[j])
            if not last:
                f_ctx = modulate(rms_norm(h_ctx, norm2_g[layer]), csh2, csc2)
                h_ctx = h_ctx + cg2 * swiglu(f_ctx, ffn_w_gate[j], ffn_w_up[j], ffn_w_down[j])
        else:
            h_lat = h_lat + g2 * moe_swiglu(f_lat, moe_router[j], moe_w_gate[j], moe_w_up[j], moe_w_down[j])
            if not last:
                f_ctx = modulate(rms_norm(h_ctx, norm2_g[layer]), csh2, csc2)
                h_ctx = h_ctx + cg2 * moe_swiglu(f_ctx, moe_router[j], moe_w_gate[j], moe_w_up[j], moe_w_down[j])
    return h_lat
```

```python
import functools

import numpy as np
import jax
import jax.numpy as jnp
from jax import lax
from jax.experimental import pallas as pl
from jax.experimental.pallas import tpu as pltpu

F32 = jnp.float32
BF16 = jnp.bfloat16

D_MODEL = 1024
GRID_W = 64
HEAD_DIM = 64
N_HEADS = 4
GROUP_WIDTH = 256
WINDOW = 128
NA_KH = 8
NA_KW = 16
MLA_NOPE = 64
MLA_ROPE = 32
MLA_QK = MLA_NOPE + MLA_ROPE
MLA_Q_RANK = 192
MLA_KV_RANK = 128
ROPE_THETA = 10000.0
EPS = 1e-6
N_EXPERTS = 8
SCALE_HEAD = HEAD_DIM ** -0.5
SCALE_MLA = MLA_QK ** -0.5
NEG = -1e30

LANE = 128
CHUNK = 128
PROJ_W = 2304

C_AQ, C_BQ, C_CQ, C_CK, C_AK, C_BK = 0, 256, 512, 768, 1024, 1152
C_V = 1280
C_DCQ, C_DCKV, C_KR = 1792, 2048, 2176

TM_LAT = 512
TQ = 256
TQ_NBR = 128
TK = 512
NBR_KEYS = 640
MOE_TB = 1024
MOE_TF = 512
VMEM_LIMIT = 56 * 1024 * 1024


def _cparams(n_axes, vmem=None):
    return pltpu.CompilerParams(dimension_semantics=("arbitrary",) * n_axes,
                                vmem_limit_bytes=vmem)


def _dot(a, b):
    return jnp.dot(a, b, preferred_element_type=F32)


def _dot_nt(a, b):
    return lax.dot_general(a, b, (((1,), (1,)), ((), ())), preferred_element_type=F32)


def _adaln_kernel(c_ref, w_ref, b_ref, o_ref):
    c = c_ref[...]
    s = c * jax.nn.sigmoid(c)
    o_ref[0] = _dot(s.astype(BF16), w_ref[0].astype(BF16)) + b_ref[0]


def _adaln(cvec, ada_w, ada_b):
    n_layers, d, n = ada_w.shape
    tn = 1536
    return pl.pallas_call(
        _adaln_kernel,
        grid=(n_layers, n // tn),
        in_specs=[pl.BlockSpec((8, d), lambda l, j: (0, 0)),
                  pl.BlockSpec((1, d, tn), lambda l, j: (l, 0, j)),
                  pl.BlockSpec((1, 1, tn), lambda l, j: (l, 0, j))],
        out_specs=pl.BlockSpec((1, 8, tn), lambda l, j: (l, 0, j)),
        out_shape=jax.ShapeDtypeStruct((n_layers, 8, n), F32),
        compiler_params=_cparams(2),
        name="adaln",
    )(cvec, ada_w, ada_b.reshape(n_layers, 1, n))


def _group_sumsq(x, m_ref):
    x2 = x * x
    hi = x2.astype(BF16)
    lo = (x2 - hi.astype(F32)).astype(BF16)
    m = m_ref[...]
    return _dot(hi, m) + _dot(lo, m)


def _rope(t, cos, sin_a, sin_b, shift):
    w = t.shape[1]
    return t * cos + pltpu.roll(t, shift, 1) * sin_a + pltpu.roll(t, w - shift, 1) * sin_b


def _tile_lanes(x, n):
    return x if n == 1 else jnp.concatenate([x] * n, axis=1)


def _proj_kernel(h_ref, mod_ref, g_ref, win_ref, gq_ref, gk_ref, gdq_ref, gdkv_ref,
                 gdqn_ref, gdkn_ref, wuq_ref, wukv_ref, m64_ref, m128_ref, *rest,
                 use_rope, tm):
    if use_rope:
        (cos_ref, sa_ref, sb_ref, cosm_ref, sam_ref, sbm_ref,
         aq_ref, bq_ref, cq_ref, dq_ref, ak_ref, bk_ref, ck_ref, dk_ref,
         avt_ref, bvt_ref, cvt_ref, dvt_ref) = rest
    else:
        (aq_ref, bq_ref, cq_ref, dq_ref, ak_ref, bk_ref, ck_ref, dk_ref,
         avt_ref, bvt_ref, cvt_ref, dvt_ref) = rest

    x = h_ref[0]
    ms = jnp.mean(x * x, axis=-1, keepdims=True)
    xn = x * lax.rsqrt(ms + EPS) * g_ref[...]
    hm = xn * (1.0 + mod_ref[0, 1:2, :]) + mod_ref[0, 0:1, :]
    pr = _dot(hm.astype(BF16), win_ref[...])

    lane = lax.broadcasted_iota(jnp.int32, (tm, LANE), 1)
    half0 = lane < HEAD_DIM

    if use_rope:
        cos2 = _tile_lanes(cos_ref[...], 2)
        sa2 = _tile_lanes(sa_ref[...], 2)
        sb2 = _tile_lanes(sb_ref[...], 2)

    def head64(col, gain, rope):
        t = pr[:, col:col + 256]
        ss = _group_sumsq(t, m64_ref)
        t = t * lax.rsqrt(ss * (1.0 / HEAD_DIM) + EPS) * gain
        if rope and use_rope:
            t = _rope(t, cos2, sa2, sb2, 16)
        return t

    gq = gq_ref[...]
    gk = gk_ref[...]

    def store_q(ref, t, placement):
        for h in range(N_HEADS):
            blk, half = placement[h]
            tb = t[:, blk * LANE:(blk + 1) * LANE]
            keep = half0 if half == 0 else jnp.logical_not(half0)
            ref[0, h] = jnp.where(keep, tb * SCALE_HEAD, 0.0).astype(BF16)

    gqa_place = ((0, 0), (1, 0), (0, 1), (1, 1))
    mha_place = ((0, 0), (0, 1), (1, 0), (1, 1))
    store_q(aq_ref, head64(C_AQ, gq[0:1], True), gqa_place)
    store_q(bq_ref, head64(C_BQ, gq[1:2], True), gqa_place)
    store_q(cq_ref, head64(C_CQ, gq[2:3], False), mha_place)
    ck_ref[0] = head64(C_CK, gk[0:1], False).astype(BF16)
    kab = head64(C_AK, gk[1:2], True)
    ak_ref[0] = kab[:, :LANE].astype(BF16)
    bk_ref[0] = kab[:, LANE:].astype(BF16)

    def store_vt(ref, vt, n_heads):
        for h in range(n_heads):
            for c in range(tm // CHUNK):
                ref[0, h, c] = vt[h * HEAD_DIM:(h + 1) * HEAD_DIM,
                                  c * CHUNK:(c + 1) * CHUNK].astype(BF16)

    vt = pr[:, C_V:C_V + 512].T
    store_vt(avt_ref, vt[0:128], 2)
    store_vt(bvt_ref, vt[128:256], 2)
    store_vt(cvt_ref, vt[256:512], 4)

    if use_rope:
        cosm = _tile_lanes(cosm_ref[...], 2)
        sam = _tile_lanes(sam_ref[...], 2)
        sbm = _tile_lanes(sbm_ref[...], 2)

    def head96(t, gain):
        ss = _group_sumsq(t, m128_ref)
        t = t * lax.rsqrt(ss * (1.0 / MLA_QK) + EPS) * gain
        if use_rope:
            t = _rope(t, cosm, sam, sbm, 8)
        return t

    cq = pr[:, C_DCQ:C_DCQ + 256]
    cq = cq * lax.rsqrt(jnp.sum(cq * cq, axis=-1, keepdims=True) * (1.0 / MLA_Q_RANK) + EPS)
    qm = _dot((cq * gdq_ref[...]).astype(BF16), wuq_ref[...])
    gdqn = gdqn_ref[...]
    for pair in range(2):
        t = head96(qm[:, pair * 256:(pair + 1) * 256], gdqn)
        for i in range(2):
            dq_ref[0, 2 * pair + i] = (t[:, i * LANE:(i + 1) * LANE] * SCALE_MLA).astype(BF16)

    ckv = pr[:, C_DCKV:C_DCKV + LANE]
    ckv = ckv * lax.rsqrt(jnp.mean(ckv * ckv, axis=-1, keepdims=True) + EPS)
    kv = _dot((ckv * gdkv_ref[...]).astype(BF16), wukv_ref[...])
    kr2 = _tile_lanes(pr[:, C_KR:C_KR + LANE], 2)
    gdkn = gdkn_ref[...]
    for pair in range(2):
        t = head96(kv[:, pair * 256:(pair + 1) * 256] + kr2, gdkn)
        dk_ref[0, :, pair * 256:(pair + 1) * 256] = t.astype(BF16)
    store_vt(dvt_ref, kv[:, 512:768].T, 4)


def _proj(h, mod, layer_w, rope_tabs, tm):
    bsz, t_len, d = h.shape
    use_rope = rope_tabs is not None
    nt = t_len // tm
    nch = t_len // CHUNK

    def full(a):
        nd = a.ndim
        return pl.BlockSpec(a.shape, lambda b, t, nd=nd: (0,) * nd)

    consts = [layer_w[k] for k in ("g1", "w_in", "gq", "gk", "gdq", "gdkv", "gdqn", "gdkn",
                                   "w_uq", "w_ukv", "m64", "m128")]
    in_specs = [pl.BlockSpec((1, tm, d), lambda b, t: (b, t, 0)),
                pl.BlockSpec((1, 2, d), lambda b, t: (b, 0, 0))]
    in_specs += [full(a) for a in consts]
    args = [h, mod] + consts
    if use_rope:
        in_specs += [pl.BlockSpec((tm, LANE), lambda b, t: (t, 0))] * 6
        args += list(rope_tabs)

    def qspec():
        return pl.BlockSpec((1, N_HEADS, tm, LANE), lambda b, t: (b, 0, t, 0))

    def kspec(w):
        return pl.BlockSpec((1, tm, w), lambda b, t: (b, t, 0))

    def vspec(nh):
        return pl.BlockSpec((1, nh, tm // CHUNK, HEAD_DIM, CHUNK), lambda b, t: (b, 0, t, 0, 0))

    def qshape():
        return jax.ShapeDtypeStruct((bsz, N_HEADS, t_len, LANE), BF16)

    def kshape(w):
        return jax.ShapeDtypeStruct((bsz, t_len, w), BF16)

    def vshape(nh):
        return jax.ShapeDtypeStruct((bsz, nh, nch, HEAD_DIM, CHUNK), BF16)

    names = ("a_q", "b_q", "c_q", "d_q", "a_k", "b_k", "c_k", "d_k", "a_vt", "b_vt", "c_vt", "d_vt")
    out_specs = [qspec(), qspec(), qspec(), qspec(),
                 kspec(128), kspec(128), kspec(256), kspec(512),
                 vspec(2), vspec(2), vspec(4), vspec(4)]
    out_shape = [qshape(), qshape(), qshape(), qshape(),
                 kshape(128), kshape(128), kshape(256), kshape(512),
                 vshape(2), vshape(2), vshape(4), vshape(4)]
    outs = pl.pallas_call(
        functools.partial(_proj_kernel, use_rope=use_rope, tm=tm),
        grid=(bsz, nt),
        in_specs=in_specs,
        out_specs=out_specs,
        out_shape=out_shape,
        compiler_params=_cparams(2, VMEM_LIMIT),
        name="proj_lat" if use_rope else "proj_ctx",
    )(*args)
    return dict(zip(names, outs))


def _flash_kernel(*refs, mode, has_sink, tq, t_lat, koffs, vsel):
    refs = list(refs)
    sink_ref = refs.pop(0) if has_sink else None
    q_ref, kc_ref, vc_ref = refs[0], refs[1], refs[2]
    refs = refs[3:]
    kl_ref = vl_ref = bm_ref = None
    if mode != "ctx":
        kl_ref, vl_ref = refs[0], refs[1]
        refs = refs[2:]
    if mode == "nbr":
        bm_ref = refs[0]
        refs = refs[1:]
    o_ref, m_scr, l_scr, acc_scr = refs

    pair = pl.program_id(1)
    qi = pl.program_id(2)

    for i in range(2):
        if has_sink:
            m_scr[i] = jnp.full((1, tq), sink_ref[2 * pair + i], F32)
            l_scr[i] = jnp.ones((1, tq), F32)
        else:
            m_scr[i] = jnp.full((1, tq), NEG, F32)
            l_scr[i] = jnp.zeros((1, tq), F32)
        acc_scr[i] = jnp.zeros((HEAD_DIM, tq), F32)

    def update(i, s_t, v_t):
        m_old = m_scr[i]
        m_new = jnp.maximum(m_old, jnp.max(s_t, axis=0, keepdims=True))
        alpha = jnp.exp(m_old - m_new)
        p = jnp.exp(s_t - m_new)
        l_scr[i] = alpha * l_scr[i] + jnp.sum(p, axis=0, keepdims=True)
        acc_scr[i] = alpha * acc_scr[i] + _dot(v_t, p.astype(BF16))
        m_scr[i] = m_new

    def v_chunks(ref, head, c0, n):
        return jnp.concatenate([ref[0, head, c0 + c] for c in range(n)], axis=1)

    n_ctx = kc_ref.shape[1]
    for i in range(2):
        s_t = _dot_nt(kc_ref[0, :, koffs[i]:koffs[i] + LANE], q_ref[0, i])
        update(i, s_t, v_chunks(vc_ref, vsel[i], 0, n_ctx // CHUNK))

    if mode == "dense":
        def body(j, carry):
            r0 = pl.multiple_of(j * TK, TK)
            for i in range(2):
                s_t = _dot_nt(kl_ref[0, pl.ds(r0, TK), koffs[i]:koffs[i] + LANE], q_ref[0, i])
                update(i, s_t, v_chunks(vl_ref, vsel[i], j * (TK // CHUNK), TK // CHUNK))
            return carry
        lax.fori_loop(0, t_lat // TK, body, 0)
    elif mode == "window":
        n_keys = tq + 2 * WINDOW
        q0 = qi * tq
        start = pl.multiple_of(jnp.clip(q0 - WINDOW, 0, t_lat - n_keys), CHUNK)
        kpos = start + lax.broadcasted_iota(jnp.int32, (n_keys, tq), 0)
        qpos = q0 + lax.broadcasted_iota(jnp.int32, (n_keys, tq), 1)
        valid = jnp.abs(kpos - qpos) <= WINDOW
        for i in range(2):
            s_t = _dot_nt(kl_ref[0, pl.ds(start, n_keys), koffs[i]:koffs[i] + LANE], q_ref[0, i])
            s_t = jnp.where(valid, s_t, NEG)
            update(i, s_t, v_chunks(vl_ref, vsel[i], start // CHUNK, n_keys // CHUNK))
    elif mode == "nbr":
        rows = t_lat // GRID_W
        start_row = jnp.clip(2 * qi - NA_KH // 2, 0, rows - NBR_KEYS // GRID_W)
        start = pl.multiple_of(start_row * GRID_W, CHUNK)
        for i in range(2):
            s_t = _dot_nt(kl_ref[0, pl.ds(start, NBR_KEYS), koffs[i]:koffs[i] + LANE], q_ref[0, i])
            s_t = s_t + bm_ref[0, i]
            update(i, s_t, v_chunks(vl_ref, vsel[i], start // CHUNK, NBR_KEYS // CHUNK))

    o_t = jnp.concatenate([acc_scr[i] / l_scr[i] for i in range(2)], axis=0)
    o_ref[0] = o_t.T


def _flash(q, k_ctx, vt_ctx, k_lat, vt_lat, *, mode, kind, sink=None, bias=None):
    bsz, _, t_q, _ = q.shape
    n_ctx = k_ctx.shape[1]
    tq = TQ_NBR if mode == "nbr" else min(TQ, t_q)
    nq = t_q // tq
    t_lat = 0 if k_lat is None else k_lat.shape[1]
    if kind == "gqa":
        kblk, kidx, koffs, vblk, vsel = LANE, (lambda p: 0), (0, 0), 1, (0, 0)
    elif kind == "mha":
        kblk, kidx, koffs, vblk, vsel = LANE, (lambda p: p), (0, 0), 2, (0, 1)
    else:
        kblk, kidx, koffs, vblk, vsel = 2 * LANE, (lambda p: p), (0, LANE), 2, (0, 1)

    in_specs, args = [], []
    if sink is not None:
        in_specs.append(pl.BlockSpec(memory_space=pltpu.SMEM))
        args.append(sink)
    in_specs.append(pl.BlockSpec((1, 2, tq, LANE), lambda b, p, i: (b, p, i, 0)))
    args.append(q)

    def kv_specs(t_len):
        return [pl.BlockSpec((1, t_len, kblk), lambda b, p, i: (b, 0, kidx(p))),
                pl.BlockSpec((1, vblk, t_len // CHUNK, HEAD_DIM, CHUNK),
                             lambda b, p, i: (b, p, 0, 0, 0))]

    in_specs += kv_specs(n_ctx)
    args += [k_ctx, vt_ctx]
    if mode != "ctx":
        in_specs += kv_specs(t_lat)
        args += [k_lat, vt_lat]
    if mode == "nbr":
        def variant(i):
            return jnp.where(i == 0, 0, jnp.where(i == 1, 1,
                             jnp.where(i == nq - 2, 3, jnp.where(i == nq - 1, 4, 2))))
        in_specs.append(pl.BlockSpec((1, 2, NBR_KEYS, tq), lambda b, p, i: (variant(i), p, 0, 0)))
        args.append(bias)

    return pl.pallas_call(
        functools.partial(_flash_kernel, mode=mode, has_sink=sink is not None, tq=tq,
                          t_lat=t_lat, koffs=koffs, vsel=vsel),
        grid=(bsz, 2, nq),
        in_specs=in_specs,
        out_specs=pl.BlockSpec((1, tq, LANE), lambda b, p, i: (b, i, p)),
        out_shape=jax.ShapeDtypeStruct((bsz, t_q, GROUP_WIDTH), F32),
        scratch_shapes=[pltpu.VMEM((2, 1, tq), F32), pltpu.VMEM((2, 1, tq), F32),
                        pltpu.VMEM((2, HEAD_DIM, tq), F32)],
        compiler_params=_cparams(3, VMEM_LIMIT),
        name=f"attn_{mode}_{kind}",
    )(*args)


def _merge_kernel(h_ref, oa_ref, ob_ref, oc_ref, od_ref, g_ref, w_ref, gate_ref, o_ref):
    acc = None
    for gi, ref in enumerate((oa_ref, ob_ref, oc_ref, od_ref)):
        y = ref[0]
        y = y * lax.rsqrt(jnp.mean(y * y, axis=-1, keepdims=True) + EPS) * g_ref[gi:gi + 1, :]
        part = _dot(y.astype(BF16), w_ref[gi * GROUP_WIDTH:(gi + 1) * GROUP_WIDTH, :])
        acc = part if acc is None else acc + part
    o_ref[0] = h_ref[0] + gate_ref[0] * acc


def _merge(h, outs, gain, w_out, gate, tm):
    bsz, t_len, d = h.shape
    ospec = pl.BlockSpec((1, tm, GROUP_WIDTH), lambda b, t: (b, t, 0))
    return pl.pallas_call(
        _merge_kernel,
        grid=(bsz, t_len // tm),
        in_specs=[pl.BlockSpec((1, tm, d), lambda b, t: (b, t, 0)), ospec, ospec, ospec, ospec,
                  pl.BlockSpec((N_HEADS, GROUP_WIDTH), lambda b, t: (0, 0)),
                  pl.BlockSpec((d, d), lambda b, t: (0, 0)),
                  pl.BlockSpec((1, 1, d), lambda b, t: (b, 0, 0))],
        out_specs=pl.BlockSpec((1, tm, d), lambda b, t: (b, t, 0)),
        out_shape=jax.ShapeDtypeStruct(h.shape, F32),
        compiler_params=_cparams(2, VMEM_LIMIT),
        name="merge_out",
    )(h, *outs, gain, w_out, gate)


def _norm_mod(x, g, mod_ref):
    xn = x * lax.rsqrt(jnp.mean(x * x, axis=-1, keepdims=True) + EPS) * g
    return xn * (1.0 + mod_ref[0, 1:2, :]) + mod_ref[0, 0:1, :]


def _ffn_kernel(h_ref, mod_ref, g_ref, wg_ref, wu_ref, wd_ref, o_ref, *, n_chunks, fc):
    x = h_ref[0]
    f = _norm_mod(x, g_ref[...], mod_ref).astype(BF16)
    acc = None
    for c in range(n_chunks):
        gate = _dot(f, wg_ref[:, c * fc:(c + 1) * fc])
        up = _dot(f, wu_ref[:, c * fc:(c + 1) * fc])
        act = (gate * jax.nn.sigmoid(gate) * up).astype(BF16)
        part = _dot(act, wd_ref[c * fc:(c + 1) * fc, :])
        acc = part if acc is None else acc + part
    o_ref[0] = x + mod_ref[0, 2:3, :] * acc


def _ffn(h, mod, gain, w_gate, w_up, w_down, tm):
    bsz, t_len, d = h.shape
    d_ff = w_gate.shape[1]
    n_chunks = 2
    fc = d_ff // n_chunks
    return pl.pallas_call(
        functools.partial(_ffn_kernel, n_chunks=n_chunks, fc=fc),
        grid=(bsz, t_len // tm),
        in_specs=[pl.BlockSpec((1, tm, d), lambda b, t: (b, t, 0)),
                  pl.BlockSpec((1, 3, d), lambda b, t: (b, 0, 0)),
                  pl.BlockSpec((1, d), lambda b, t: (0, 0)),
                  pl.BlockSpec((d, d_ff), lambda b, t: (0, 0), pipeline_mode=pl.Buffered(1)),
                  pl.BlockSpec((d, d_ff), lambda b, t: (0, 0), pipeline_mode=pl.Buffered(1)),
                  pl.BlockSpec((d_ff, d), lambda b, t: (0, 0), pipeline_mode=pl.Buffered(1))],
        out_specs=pl.BlockSpec((1, tm, d), lambda b, t: (b, t, 0)),
        out_shape=jax.ShapeDtypeStruct(h.shape, F32),
        compiler_params=_cparams(2, VMEM_LIMIT),
        name="ffn_dense",
    )(h, mod, gain, w_gate, w_up, w_down)


def _route_kernel(h_ref, mod_ref, g_ref, rt_ref, f_ref, e_ref, gate_ref, rank_ref, cnt_ref,
                  carry_scr, *, tm):
    step = pl.program_id(0) * pl.num_programs(1) + pl.program_id(1)

    @pl.when(step == 0)
    def _():
        carry_scr[...] = jnp.zeros_like(carry_scr)

    f = _norm_mod(h_ref[0], g_ref[...], mod_ref)
    f_ref[0] = f
    logits = _dot_nt(rt_ref[...], f.astype(BF16))
    row = lax.broadcasted_iota(jnp.int32, (N_EXPERTS, tm), 0)
    m1 = jnp.max(logits, axis=0, keepdims=True)
    e1 = jnp.min(jnp.where(logits == m1, row, N_EXPERTS), axis=0, keepdims=True)
    rest = jnp.where(row == e1, -jnp.inf, logits)
    m2 = jnp.max(rest, axis=0, keepdims=True)
    e2 = jnp.min(jnp.where(rest == m2, row, N_EXPERTS), axis=0, keepdims=True)
    z = jnp.exp(m2 - m1)
    den = 1.0 + z
    gate_ref[0, 0:1, :] = 1.0 / den
    gate_ref[0, 1:2, :] = z / den
    e_ref[0, 0:1, :] = e1
    e_ref[0, 1:2, :] = e2

    ind = jnp.logical_or(row == e1, row == e2)
    ii = lax.broadcasted_iota(jnp.int32, (tm, tm), 0)
    jj = lax.broadcasted_iota(jnp.int32, (tm, tm), 1)
    upper = jnp.where(ii < jj, 1.0, 0.0).astype(BF16)
    before = _dot(jnp.where(ind, 1.0, 0.0).astype(BF16), upper) + carry_scr[...]
    rank_ref[0, 0:1, :] = jnp.sum(jnp.where(row == e1, before, 0.0), axis=0, keepdims=True).astype(jnp.int32)
    rank_ref[0, 1:2, :] = jnp.sum(jnp.where(row == e2, before, 0.0), axis=0, keepdims=True).astype(jnp.int32)
    carry_scr[...] = carry_scr[...] + jnp.sum(jnp.where(ind, 1.0, 0.0), axis=1, keepdims=True)
    cnt_ref[...] = carry_scr[...].astype(jnp.int32)


def _route(h, mod, gain, router_t, tm):
    bsz, t_len, d = h.shape
    nt = t_len // tm
    n_tiles = bsz * nt
    small = lambda: pl.BlockSpec((1, 2, tm), lambda b, t: (b * nt + t, 0, 0))
    return pl.pallas_call(
        functools.partial(_route_kernel, tm=tm),
        grid=(bsz, nt),
        in_specs=[pl.BlockSpec((1, tm, d), lambda b, t: (b, t, 0)),
                  pl.BlockSpec((1, 3, d), lambda b, t: (b, 0, 0)),
                  pl.BlockSpec((1, d), lambda b, t: (0, 0)),
                  pl.BlockSpec((N_EXPERTS, d), lambda b, t: (0, 0))],
        out_specs=[pl.BlockSpec((1, tm, d), lambda b, t: (b, t, 0)),
                   small(), small(), small(),
                   pl.BlockSpec((N_EXPERTS, 1), lambda b, t: (0, 0))],
        out_shape=[jax.ShapeDtypeStruct(h.shape, F32),
                   jax.ShapeDtypeStruct((n_tiles, 2, tm), jnp.int32),
                   jax.ShapeDtypeStruct((n_tiles, 2, tm), F32),
                   jax.ShapeDtypeStruct((n_tiles, 2, tm), jnp.int32),
                   jax.ShapeDtypeStruct((N_EXPERTS, 1), jnp.int32)],
        scratch_shapes=[pltpu.VMEM((N_EXPERTS, 1), F32)],
        compiler_params=_cparams(2, VMEM_LIMIT),
        name="moe_route",
    )(h, mod, gain, router_t)


def _dispatch_kernel(pstart_ref, e_ref, rank_ref, f_ref, xs_in_ref, xs_ref, sem, *, tm):
    del xs_in_ref

    def row_copy(i, k):
        dest = pstart_ref[e_ref[0, k, i]] + rank_ref[0, k, i]
        return pltpu.make_async_copy(f_ref.at[pl.ds(i, 1)], xs_ref.at[pl.ds(dest, 1)], sem)

    def issue(i, carry):
        row_copy(i, 0).start()
        row_copy(i, 1).start()
        return carry
    lax.fori_loop(0, tm, issue, 0)
    for _ in range(2):
        pltpu.make_async_copy(f_ref, xs_ref.at[pl.ds(0, tm)], sem).wait()


def _dispatch(f_flat, e_idx, rank, pstart, n_rows, tm):
    n_tok, d = f_flat.shape
    xs0 = jnp.zeros((n_rows, d), F32)
    smem = lambda: pl.BlockSpec((1, 2, tm), lambda t, ps: (t, 0, 0), memory_space=pltpu.SMEM)
    return pl.pallas_call(
        functools.partial(_dispatch_kernel, tm=tm),
        grid_spec=pltpu.PrefetchScalarGridSpec(
            num_scalar_prefetch=1,
            grid=(n_tok // tm,),
            in_specs=[smem(), smem(),
                      pl.BlockSpec((tm, d), lambda t, ps: (t, 0)),
                      pl.BlockSpec(memory_space=pl.ANY)],
            out_specs=pl.BlockSpec(memory_space=pl.ANY),
            scratch_shapes=[pltpu.SemaphoreType.DMA(())]),
        out_shape=jax.ShapeDtypeStruct((n_rows, d), F32),
        input_output_aliases={4: 0},
        compiler_params=pltpu.CompilerParams(dimension_semantics=("arbitrary",),
                                             has_side_effects=True),
        name="moe_dispatch",
    )(pstart, e_idx, rank, f_flat, xs0)


def _moe_kernel(be_ref, nrow_ref, nused_ref, x_ref, wg_ref, wu_ref, wd_ref, y_ref, *, tb, sub):
    del be_ref, nused_ref
    blk = pl.program_id(0)
    fi = pl.program_id(1)
    n_valid = nrow_ref[blk]

    for s in range(tb // sub):
        @pl.when(n_valid > s * sub)
        def _(s=s):
            x = x_ref[s * sub:(s + 1) * sub, :].astype(BF16)
            gate = _dot(x, wg_ref[0].astype(BF16))
            up = _dot(x, wu_ref[0].astype(BF16))
            act = (gate * jax.nn.sigmoid(gate) * up).astype(BF16)
            part = _dot(act, wd_ref[0].astype(BF16))

            @pl.when(fi == 0)
            def _():
                y_ref[s * sub:(s + 1) * sub, :] = part

            @pl.when(fi != 0)
            def _():
                y_ref[s * sub:(s + 1) * sub, :] += part

        @pl.when(jnp.logical_and(n_valid <= s * sub, fi == 0))
        def _(s=s):
            y_ref[s * sub:(s + 1) * sub, :] = jnp.zeros((sub, y_ref.shape[1]), F32)


def _moe_blocks(xs, block_e, block_rows, n_used, w_gate, w_up, w_down):
    n_rows, d = xs.shape
    nb = n_rows // MOE_TB
    d_ff = w_gate.shape[2]
    nf = d_ff // MOE_TF

    def bidx(b, be, nr, nu):
        return jnp.minimum(b, nu[0] - 1)

    def fidx(b, f, nu):
        return jnp.where(b < nu[0], f, nf - 1)

    return pl.pallas_call(
        functools.partial(_moe_kernel, tb=MOE_TB, sub=MOE_TB // 2),
        grid_spec=pltpu.PrefetchScalarGridSpec(
            num_scalar_prefetch=3,
            grid=(nb, nf),
            in_specs=[pl.BlockSpec((MOE_TB, d), lambda b, f, be, nr, nu: (bidx(b, be, nr, nu), 0)),
                      pl.BlockSpec((1, d, MOE_TF),
                                   lambda b, f, be, nr, nu: (be[bidx(b, be, nr, nu)], 0, fidx(b, f, nu))),
                      pl.BlockSpec((1, d, MOE_TF),
                                   lambda b, f, be, nr, nu: (be[bidx(b, be, nr, nu)], 0, fidx(b, f, nu))),
                      pl.BlockSpec((1, MOE_TF, d),
                                   lambda b, f, be, nr, nu: (be[bidx(b, be, nr, nu)], fidx(b, f, nu), 0))],
            out_specs=pl.BlockSpec((MOE_TB, d), lambda b, f, be, nr, nu: (b, 0))),
        out_shape=jax.ShapeDtypeStruct((n_rows, d), F32),
        compiler_params=_cparams(2, VMEM_LIMIT),
        name="moe_experts",
    )(block_e, block_rows, n_used, xs, w_gate, w_up, w_down)


def _combine_kernel(pstart_ref, e_ref, rank_ref, h_ref, g2_ref, gates_ref, y_ref, o_ref,
                    buf0, buf1, sem, *, tm):
    def row_copy(i, k, buf):
        src = pstart_ref[e_ref[0, k, i]] + rank_ref[0, k, i]
        return pltpu.make_async_copy(y_ref.at[pl.ds(src, 1)], buf.at[pl.ds(i, 1)], sem)

    def issue(i, carry):
        row_copy(i, 0, buf0).start()
        row_copy(i, 1, buf1).start()
        return carry
    lax.fori_loop(0, tm, issue, 0)
    pltpu.make_async_copy(y_ref.at[pl.ds(0, tm)], buf0, sem).wait()
    pltpu.make_async_copy(y_ref.at[pl.ds(0, tm)], buf1, sem).wait()

    gates = gates_ref[...]
    moe = buf0[...] * gates[:, 0:1] + buf1[...] * gates[:, 1:2]
    o_ref[0] = h_ref[0] + g2_ref[0] * moe


def _combine(h, gate2, gates_col, y, e_idx, rank, pstart, tm):
    bsz, t_len, d = h.shape
    nt = t_len // tm
    smem = lambda: pl.BlockSpec((1, 2, tm), lambda b, t, ps: (b * nt + t, 0, 0),
                                memory_space=pltpu.SMEM)
    return pl.pallas_call(
        functools.partial(_combine_kernel, tm=tm),
        grid_spec=pltpu.PrefetchScalarGridSpec(
            num_scalar_prefetch=1,
            grid=(bsz, nt),
            in_specs=[smem(), smem(),
                      pl.BlockSpec((1, tm, d), lambda b, t, ps: (b, t, 0)),
                      pl.BlockSpec((1, 1, d), lambda b, t, ps: (b, 0, 0)),
                      pl.BlockSpec((tm, 2), lambda b, t, ps: (b * nt + t, 0)),
                      pl.BlockSpec(memory_space=pl.ANY)],
            out_specs=pl.BlockSpec((1, tm, d), lambda b, t, ps: (b, t, 0)),
            scratch_shapes=[pltpu.VMEM((tm, d), F32), pltpu.VMEM((tm, d), F32),
                            pltpu.SemaphoreType.DMA(())]),
        out_shape=jax.ShapeDtypeStruct(h.shape, F32),
        compiler_params=_cparams(2, VMEM_LIMIT),
        name="moe_combine",
    )(pstart, e_idx, rank, h, gate2, gates_col, y)


def _moe(h, mod, gain, router, w_gate, w_up, w_down):
    bsz, t_len, d = h.shape
    n_tok = bsz * t_len
    tm = min(TM_LAT, t_len)
    f, e_idx, gates, rank, counts = _route(h, mod, gain, router.T.astype(BF16), tm)

    counts = counts[:, 0]
    padded = (counts + MOE_TB - 1) // MOE_TB * MOE_TB
    pends = jnp.cumsum(padded)
    pstart = (pends - padded).astype(jnp.int32)
    nb = (2 * n_tok + N_EXPERTS * (MOE_TB - 1) + MOE_TB - 1) // MOE_TB
    bstart = jnp.arange(nb, dtype=jnp.int32) * MOE_TB
    block_e = jnp.clip(jnp.searchsorted(pends, bstart, side="right"), 0, N_EXPERTS - 1).astype(jnp.int32)
    block_rows = jnp.clip(pstart[block_e] + counts[block_e] - bstart, 0, MOE_TB).astype(jnp.int32)
    n_used = (pends[-1] // MOE_TB).astype(jnp.int32).reshape(1)

    xs = _dispatch(f.reshape(n_tok, d), e_idx, rank, pstart, nb * MOE_TB, tm)
    y = _moe_blocks(xs, block_e, block_rows, n_used, w_gate, w_up, w_down)
    gates_col = gates.transpose(0, 2, 1).reshape(n_tok, 2)
    return _combine(h, mod[:, 2:3, :], gates_col, y, e_idx, rank, pstart, tm)


def _rope_tables(n_tokens, rot_dim):
    t = jnp.arange(n_tokens)
    row = (t // GRID_W).astype(F32)[:, None]
    col = (t % GRID_W).astype(F32)[:, None]
    half = rot_dim // 2
    inv_freq = ROPE_THETA ** (-jnp.arange(0, half, 2, dtype=F32) / half)
    ang = jnp.concatenate([row * inv_freq, row * inv_freq, col * inv_freq, col * inv_freq], axis=-1)
    return jnp.cos(ang), jnp.sin(ang)


def _rope_inputs(n_tokens):
    cos64, sin64 = _rope_tables(n_tokens, HEAD_DIM)
    lane = np.arange(HEAD_DIM)
    upper = jnp.asarray((lane % 32) >= 16)
    cos = jnp.tile(cos64, (1, 2))
    sin_a = jnp.tile(jnp.where(upper, sin64, 0.0), (1, 2))
    sin_b = jnp.tile(jnp.where(upper, 0.0, -sin64), (1, 2))

    cos32, sin32 = _rope_tables(n_tokens, MLA_ROPE)
    lane = np.arange(MLA_ROPE)
    upper = jnp.asarray((lane % 16) >= 8)
    ones = jnp.ones((n_tokens, MLA_NOPE), F32)
    zeros = jnp.zeros((n_tokens, MLA_NOPE), F32)
    pad1 = jnp.ones((n_tokens, LANE - MLA_QK), F32)
    pad0 = jnp.zeros((n_tokens, LANE - MLA_QK), F32)
    cos_m = jnp.concatenate([ones, cos32, pad1], axis=1)
    sin_am = jnp.concatenate([zeros, jnp.where(upper, sin32, 0.0), pad0], axis=1)
    sin_bm = jnp.concatenate([zeros, jnp.where(upper, 0.0, -sin32), pad0], axis=1)
    return cos, sin_a, sin_b, cos_m, sin_am, sin_bm


def _block_diag_ones(width, block):
    idx = np.arange(width) // block
    return jnp.asarray(idx[:, None] == idx[None, :], dtype=BF16)


def _layer_weights(l, norm1_g, w_in, a_qn, a_kn, b_qn, b_kn, c_qn, c_kn, d_q_norm, d_w_uq,
                   d_kv_norm, d_w_ukv, d_qn, d_kn):
    w = w_in[l]
    d = w.shape[0]

    def cols(a, b):
        return w[:, a:b]

    def perm_q(a):
        q = cols(a, a + 256).reshape(d, 4, HEAD_DIM)
        return q[:, jnp.array([0, 2, 1, 3]), :].reshape(d, 256)

    z = lambda n: jnp.zeros((d, n), w.dtype)
    w_new = jnp.concatenate([
        perm_q(0), perm_q(512), cols(1024, 1280), cols(1280, 1536),
        cols(256, 384), cols(768, 896),
        cols(384, 512), cols(896, 1024), cols(1536, 1792),
        cols(1792, 1984), z(64),
        cols(1984, 2112),
        z(MLA_NOPE), cols(2112, 2144), z(LANE - MLA_QK),
    ], axis=1).astype(BF16)

    t4 = lambda g: jnp.tile(g[l], 4)
    gq = jnp.stack([t4(a_qn), t4(b_qn), t4(c_qn)])
    gk = jnp.stack([t4(c_kn), jnp.concatenate([jnp.tile(a_kn[l], 2), jnp.tile(b_kn[l], 2)])])
    pad96 = lambda g: jnp.tile(jnp.concatenate([g[l], jnp.zeros((LANE - MLA_QK,), F32)]), 2)[None, :]
    gdq = jnp.concatenate([d_q_norm[l], jnp.zeros((256 - MLA_Q_RANK,), F32)])[None, :]

    wuq = d_w_uq[l].reshape(MLA_Q_RANK, N_HEADS, MLA_QK)
    wuq = jnp.pad(wuq, ((0, 256 - MLA_Q_RANK), (0, 0), (0, LANE - MLA_QK))).reshape(256, N_HEADS * LANE)
    wukv = d_w_ukv[l].reshape(MLA_KV_RANK, N_HEADS, MLA_NOPE + HEAD_DIM)
    wk = jnp.pad(wukv[:, :, :MLA_NOPE], ((0, 0), (0, 0), (0, LANE - MLA_NOPE))).reshape(MLA_KV_RANK, N_HEADS * LANE)
    wv = wukv[:, :, MLA_NOPE:].reshape(MLA_KV_RANK, N_HEADS * HEAD_DIM)
    return dict(
        g1=norm1_g[l][None, :], w_in=w_new, gq=gq, gk=gk, gdq=gdq, gdkv=d_kv_norm[l][None, :],
        gdqn=pad96(d_qn), gdkn=pad96(d_kn),
        w_uq=wuq.astype(BF16), w_ukv=jnp.concatenate([wk, wv], axis=1).astype(BF16),
        m64=_block_diag_ones(256, HEAD_DIM), m128=_block_diag_ones(256, LANE))


def _nbr_index_tables(rows):
    rep_rows = (0, 2, 4, rows - 4, rows - 2)
    n_krow = NBR_KEYS // GRID_W
    dr = np.zeros((5, n_krow, 2), np.int64)
    vr = np.zeros((5, n_krow, 2), bool)
    for v, r in enumerate(rep_rows):
        start_row = min(max(r - NA_KH // 2, 0), rows - n_krow)
        for a in range(n_krow):
            for q in range(2):
                qr, kr = r + q, start_row + a
                rs = min(max(qr - NA_KH // 2, 0), rows - NA_KH)
                vr[v, a, q] = rs <= kr < rs + NA_KH
                dr[v, a, q] = kr - qr + NA_KH - 1
    j = np.arange(GRID_W)
    cs = np.clip(j - NA_KW // 2, 0, GRID_W - NA_KW)
    kc = np.arange(GRID_W)[:, None]
    vc = (kc >= cs[None, :]) & (kc < cs[None, :] + NA_KW)
    dc = kc - j[None, :] + NA_KW - 1
    valid = vr[:, :, None, :, None] & vc[None, None, :, None, :]
    dr_full = np.broadcast_to(dr[:, :, None, :, None], valid.shape)
    dc_full = np.broadcast_to(dc[None, None, :, None, :], valid.shape)
    shape = (5, NBR_KEYS, TQ_NBR)
    return (np.where(valid, dr_full, 0).reshape(shape), np.where(valid, dc_full, 0).reshape(shape),
            valid.reshape(shape))


def _nbr_bias(rpb, rows):
    dr, dc, valid = _nbr_index_tables(rows)
    tab = rpb[:, dr, dc]
    tab = jnp.where(valid[None], tab, NEG)
    return tab.transpose(1, 0, 2, 3)


def _mixers(h_lat, h_ctx, mod_lat, mod_ctx, lw, rope_tabs, a_sink, nbr_bias, with_ctx_out):
    lat = _proj(h_lat, mod_lat, lw, rope_tabs, TM_LAT)
    cx = _proj(h_ctx, mod_ctx, lw, None, h_ctx.shape[1])

    def attend(q, name, kind, mode, **kw):
        if mode == "ctx":
            return _flash(q[name + "_q"], cx[name + "_k"], cx[name + "_vt"], None, None,
                          mode=mode, kind=kind, **kw)
        return _flash(q[name + "_q"], cx[name + "_k"], cx[name + "_vt"], lat[name + "_k"],
                      lat[name + "_vt"], mode=mode, kind=kind, **kw)

    outs_lat = (attend(lat, "a", "gqa", "window", sink=a_sink),
                attend(lat, "b", "gqa", "dense"),
                attend(lat, "c", "mha", "nbr", bias=nbr_bias),
                attend(lat, "d", "mla", "dense"))
    outs_ctx = None
    if with_ctx_out:
        outs_ctx = (attend(cx, "a", "gqa", "ctx", sink=a_sink),
                    attend(cx, "b", "gqa", "ctx"),
                    attend(cx, "c", "mha", "ctx"),
                    attend(cx, "d", "mla", "ctx"))
    return outs_lat, outs_ctx


def kernel(x, c, ctx, c_ctx, ada_w, ada_b, norm1_g, norm2_g, w_in, a_qn, a_kn, a_sink, b_qn, b_kn,
           c_qn, c_kn, c_rpb, d_q_norm, d_w_uq, d_kv_norm, d_w_ukv, d_qn, d_kn, out_norm_g, w_out,
           ffn_w_gate, ffn_w_up, ffn_w_down, moe_router, moe_w_gate, moe_w_up, moe_w_down):
    bsz, seq, d = x.shape
    depth = ada_w.shape[0]
    n_ctx = ctx.shape[1]
    assert d == D_MODEL and seq % TM_LAT == 0 and seq >= NBR_KEYS and n_ctx % TQ == 0
    assert (seq // GRID_W) % 2 == 0 and seq // TQ_NBR >= 5

    cvec = jnp.concatenate([c, c_ctx[None, :], jnp.zeros((8 - bsz - 1, d), F32)], axis=0)
    mod_all = _adaln(cvec, ada_w, ada_b)
    rope_tabs = _rope_inputs(seq)

    h_lat, h_ctx = x, ctx
    for layer in range(depth):
        last = layer == depth - 1
        mod = mod_all[layer].reshape(8, 6, d)
        mod_lat = mod[:bsz]
        mod_ctx = jnp.broadcast_to(mod[bsz:bsz + 1], (bsz, 6, d))
        lw = _layer_weights(layer, norm1_g, w_in, a_qn, a_kn, b_qn, b_kn, c_qn, c_kn, d_q_norm,
                            d_w_uq, d_kv_norm, d_w_ukv, d_qn, d_kn)
        nbr_bias = _nbr_bias(c_rpb[layer], seq // GRID_W)
        outs_lat, outs_ctx = _mixers(h_lat, h_ctx, mod_lat[:, 0:2], mod_ctx[:, 0:2], lw, rope_tabs,
                                     a_sink[layer], nbr_bias, not last)
        gain_o = out_norm_g[layer].reshape(N_HEADS, GROUP_WIDTH)
        w_o = w_out[layer].astype(BF16)
        h_lat = _merge(h_lat, outs_lat, gain_o, w_o, mod_lat[:, 2:3], TM_LAT)
        if not last:
            h_ctx = _merge(h_ctx, outs_ctx, gain_o, w_o, mod_ctx[:, 2:3], n_ctx)

        j = layer // 2
        g2 = norm2_g[layer][None, :]
        if layer % 2 == 0:
            wg, wu, wd = (ffn_w_gate[j].astype(BF16), ffn_w_up[j].astype(BF16),
                          ffn_w_down[j].astype(BF16))
            h_lat = _ffn(h_lat, mod_lat[:, 3:6], g2, wg, wu, wd, TM_LAT)
            if not last:
                h_ctx = _ffn(h_ctx, mod_ctx[:, 3:6], g2, wg, wu, wd, n_ctx)
        else:
            h_lat = _moe(h_lat, mod_lat[:, 3:6], g2, moe_router[j], moe_w_gate[j], moe_w_up[j],
                         moe_w_down[j])
            if not last:
                h_ctx = _moe(h_ctx, mod_ctx[:, 3:6], g2, moe_router[j], moe_w_gate[j], moe_w_up[j],
                             moe_w_down[j])
    return h_lat
```

```python
import functools

import numpy as np
import jax
import jax.numpy as jnp
from jax import lax
from jax.experimental import pallas as pl
from jax.experimental.pallas import tpu as pltpu

F32 = jnp.float32
BF16 = jnp.bfloat16

D_MODEL = 1024
GRID_W = 64
HEAD_DIM = 64
N_HEADS = 4
GROUP_WIDTH = 256
WINDOW = 128
NA_KH = 8
NA_KW = 16
MLA_NOPE = 64
MLA_ROPE = 32
MLA_QK = MLA_NOPE + MLA_ROPE
MLA_Q_RANK = 192
MLA_KV_RANK = 128
ROPE_THETA = 10000.0
EPS = 1e-6
N_EXPERTS = 8
SCALE_HEAD = HEAD_DIM ** -0.5
SCALE_MLA = MLA_QK ** -0.5
NEG = -1e30

LANE = 128
CHUNK = 128
PROJ_W = 2304

C_AQ, C_BQ, C_CQ, C_CK, C_AK, C_BK = 0, 256, 512, 768, 1024, 1152
C_V = 1280
C_DCQ, C_DCKV, C_KR = 1792, 2048, 2176

TM_LAT = 512
TQ = 256
TQ_NBR = 128
TK = 512
NBR_KEYS = 640
MOE_TB = 1024
MOE_TF = 512
VMEM_LIMIT = 56 * 1024 * 1024


def _cparams(n_axes, vmem=None):
    return pltpu.CompilerParams(dimension_semantics=("arbitrary",) * n_axes,
                                vmem_limit_bytes=vmem)


def _dot(a, b):
    return jnp.dot(a, b, preferred_element_type=F32)


def _dot_nt(a, b):
    return lax.dot_general(a, b, (((1,), (1,)), ((), ())), preferred_element_type=F32)


def _adaln_kernel(c_ref, w_ref, b_ref, o_ref):
    c = c_ref[...]
    s = c * jax.nn.sigmoid(c)
    o_ref[0] = _dot(s.astype(BF16), w_ref[0].astype(BF16)) + b_ref[0]


def _adaln(cvec, ada_w, ada_b):
    n_layers, d, n = ada_w.shape
    tn = 1536
    return pl.pallas_call(
        _adaln_kernel,
        grid=(n_layers, n // tn),
        in_specs=[pl.BlockSpec((8, d), lambda l, j: (0, 0)),
                  pl.BlockSpec((1, d, tn), lambda l, j: (l, 0, j)),
                  pl.BlockSpec((1, 1, tn), lambda l, j: (l, 0, j))],
        out_specs=pl.BlockSpec((1, 8, tn), lambda l, j: (l, 0, j)),
        out_shape=jax.ShapeDtypeStruct((n_layers, 8, n), F32),
        compiler_params=_cparams(2),
        name="adaln",
    )(cvec, ada_w, ada_b.reshape(n_layers, 1, n))


def _group_sumsq(x, m_ref):
    x2 = x * x
    hi = x2.astype(BF16)
    lo = (x2 - hi.astype(F32)).astype(BF16)
    m = m_ref[...]
    return _dot(hi, m) + _dot(lo, m)


def _rope(t, cos, sin_a, sin_b, shift):
    w = t.shape[1]
    return t * cos + pltpu.roll(t, shift, 1) * sin_a + pltpu.roll(t, w - shift, 1) * sin_b


def _tile_lanes(x, n):
    return x if n == 1 else jnp.concatenate([x] * n, axis=1)


def _proj_kernel(h_ref, mod_ref, g_ref, win_ref, gq_ref, gk_ref, gdq_ref, gdkv_ref,
                 gdqn_ref, gdkn_ref, wuq_ref, wukv_ref, m64_ref, m128_ref, *rest,
                 use_rope, tm):
    if use_rope:
        (cos_ref, sa_ref, sb_ref, cosm_ref, sam_ref, sbm_ref,
         aq_ref, bq_ref, cq_ref, dq_ref, ak_ref, bk_ref, ck_ref, dk_ref,
         avt_ref, bvt_ref, cvt_ref, dvt_ref) = rest
    else:
        (aq_ref, bq_ref, cq_ref, dq_ref, ak_ref, bk_ref, ck_ref, dk_ref,
         avt_ref, bvt_ref, cvt_ref, dvt_ref) = rest

    x = h_ref[0]
    ms = jnp.mean(x * x, axis=-1, keepdims=True)
    xn = x * lax.rsqrt(ms + EPS) * g_ref[...]
    hm = xn * (1.0 + mod_ref[0, 1:2, :]) + mod_ref[0, 0:1, :]
    pr = _dot(hm.astype(BF16), win_ref[...])

    lane = lax.broadcasted_iota(jnp.int32, (tm, LANE), 1)
    half0 = lane < HEAD_DIM

    if use_rope:
        cos2 = _tile_lanes(cos_ref[...], 2)
        sa2 = _tile_lanes(sa_ref[...], 2)
        sb2 = _tile_lanes(sb_ref[...], 2)

    def head64(col, gain, rope):
        t = pr[:, col:col + 256]
        ss = _group_sumsq(t, m64_ref)
        t = t * lax.rsqrt(ss * (1.0 / HEAD_DIM) + EPS) * gain
        if rope and use_rope:
            t = _rope(t, cos2, sa2, sb2, 16)
        return t

    gq = gq_ref[...]
    gk = gk_ref[...]

    def store_q(ref, t, placement):
        for h in range(N_HEADS):
            blk, half = placement[h]
            tb = t[:, blk * LANE:(blk + 1) * LANE]
            keep = half0 if half == 0 else jnp.logical_not(half0)
            ref[0, h] = jnp.where(keep, tb * SCALE_HEAD, 0.0).astype(BF16)

    gqa_place = ((0, 0), (1, 0), (0, 1), (1, 1))
    mha_place = ((0, 0), (0, 1), (1, 0), (1, 1))
    store_q(aq_ref, head64(C_AQ, gq[0:1], True), gqa_place)
    store_q(bq_ref, head64(C_BQ, gq[1:2], True), gqa_place)
    store_q(cq_ref, head64(C_CQ, gq[2:3], False), mha_place)
    ck_ref[0] = head64(C_CK, gk[0:1], False).astype(BF16)
    kab = head64(C_AK, gk[1:2], True)
    ak_ref[0] = kab[:, :LANE].astype(BF16)
    bk_ref[0] = kab[:, LANE:].astype(BF16)

    def store_vt(ref, vt, n_heads):
        for h in range(n_heads):
            for c in range(tm // CHUNK):
                ref[0, h, c] = vt[h * HEAD_DIM:(h + 1) * HEAD_DIM,
                                  c * CHUNK:(c + 1) * CHUNK].astype(BF16)

    vt = pr[:, C_V:C_V + 512].T
    store_vt(avt_ref, vt[0:128], 2)
    store_vt(bvt_ref, vt[128:256], 2)
    store_vt(cvt_ref, vt[256:512], 4)

    if use_rope:
        cosm = _tile_lanes(cosm_ref[...], 2)
        sam = _tile_lanes(sam_ref[...], 2)
        sbm = _tile_lanes(sbm_ref[...], 2)

    def head96(t, gain):
        ss = _group_sumsq(t, m128_ref)
        t = t * lax.rsqrt(ss * (1.0 / MLA_QK) + EPS) * gain
        if use_rope:
            t = _rope(t, cosm, sam, sbm, 8)
        return t

    cq = pr[:, C_DCQ:C_DCQ + 256]
    cq = cq * lax.rsqrt(jnp.sum(cq * cq, axis=-1, keepdims=True) * (1.0 / MLA_Q_RANK) + EPS)
    qm = _dot((cq * gdq_ref[...]).astype(BF16), wuq_ref[...])
    gdqn = gdqn_ref[...]
    for pair in range(2):
        t = head96(qm[:, pair * 256:(pair + 1) * 256], gdqn)
        for i in range(2):
            dq_ref[0, 2 * pair + i] = (t[:, i * LANE:(i + 1) * LANE] * SCALE_MLA).astype(BF16)

    ckv = pr[:, C_DCKV:C_DCKV + LANE]
    ckv = ckv * lax.rsqrt(jnp.mean(ckv * ckv, axis=-1, keepdims=True) + EPS)
    kv = _dot((ckv * gdkv_ref[...]).astype(BF16), wukv_ref[...])
    kr2 = _tile_lanes(pr[:, C_KR:C_KR + LANE], 2)
    gdkn = gdkn_ref[...]
    for pair in range(2):
        t = head96(kv[:, pair * 256:(pair + 1) * 256] + kr2, gdkn)
        dk_ref[0, :, pair * 256:(pair + 1) * 256] = t.astype(BF16)
    store_vt(dvt_ref, kv[:, 512:768].T, 4)


def _proj(h, mod, layer_w, rope_tabs, tm):
    bsz, t_len, d = h.shape
    use_rope = rope_tabs is not None
    nt = t_len // tm
    nch = t_len // CHUNK

    def full(a):
        nd = a.ndim
        return pl.BlockSpec(a.shape, lambda b, t, nd=nd: (0,) * nd)

    consts = [layer_w[k] for k in ("g1", "w_in", "gq", "gk", "gdq", "gdkv", "gdqn", "gdkn",
                                   "w_uq", "w_ukv", "m64", "m128")]
    in_specs = [pl.BlockSpec((1, tm, d), lambda b, t: (b, t, 0)),
                pl.BlockSpec((1, 2, d), lambda b, t: (b, 0, 0))]
    in_specs += [full(a) for a in consts]
    args = [h, mod] + consts
    if use_rope:
        in_specs += [pl.BlockSpec((tm, LANE), lambda b, t: (t, 0))] * 6
        args += list(rope_tabs)

    def qspec():
        return pl.BlockSpec((1, N_HEADS, tm, LANE), lambda b, t: (b, 0, t, 0))

    def kspec(w):
        return pl.BlockSpec((1, tm, w), lambda b, t: (b, t, 0))

    def vspec(nh):
        return pl.BlockSpec((1, nh, tm // CHUNK, HEAD_DIM, CHUNK), lambda b, t: (b, 0, t, 0, 0))

    def qshape():
        return jax.ShapeDtypeStruct((bsz, N_HEADS, t_len, LANE), BF16)

    def kshape(w):
        return jax.ShapeDtypeStruct((bsz, t_len, w), BF16)

    def vshape(nh):
        return jax.ShapeDtypeStruct((bsz, nh, nch, HEAD_DIM, CHUNK), BF16)

    names = ("a_q", "b_q", "c_q", "d_q", "a_k", "b_k", "c_k", "d_k", "a_vt", "b_vt", "c_vt", "d_vt")
    out_specs = [qspec(), qspec(), qspec(), qspec(),
                 kspec(128), kspec(128), kspec(256), kspec(512),
                 vspec(2), vspec(2), vspec(4), vspec(4)]
    out_shape = [qshape(), qshape(), qshape(), qshape(),
                 kshape(128), kshape(128), kshape(256), kshape(512),
                 vshape(2), vshape(2), vshape(4), vshape(4)]
    outs = pl.pallas_call(
        functools.partial(_proj_kernel, use_rope=use_rope, tm=tm),
        grid=(bsz, nt),
        in_specs=in_specs,
        out_specs=out_specs,
        out_shape=out_shape,
        compiler_params=_cparams(2, VMEM_LIMIT),
        name="proj_lat" if use_rope else "proj_ctx",
    )(*args)
    return dict(zip(names, outs))


def _flash_kernel(*refs, mode, has_sink, tq, t_lat, koffs, vsel):
    refs = list(refs)
    sink_ref = refs.pop(0) if has_sink else None
    q_ref, kc_ref, vc_ref = refs[0], refs[1], refs[2]
    refs = refs[3:]
    kl_ref = vl_ref = bm_ref = None
    if mode != "ctx":
        kl_ref, vl_ref = refs[0], refs[1]
        refs = refs[2:]
    if mode == "nbr":
        bm_ref = refs[0]
        refs = refs[1:]
    o_ref, m_scr, l_scr, acc_scr = refs

    pair = pl.program_id(1)
    qi = pl.program_id(2)

    for i in range(2):
        if has_sink:
            m_scr[i] = jnp.full((1, tq), sink_ref[2 * pair + i], F32)
            l_scr[i] = jnp.ones((1, tq), F32)
        else:
            m_scr[i] = jnp.full((1, tq), NEG, F32)
            l_scr[i] = jnp.zeros((1, tq), F32)
        acc_scr[i] = jnp.zeros((HEAD_DIM, tq), F32)

    def update(i, s_t, v_t):
        m_old = m_scr[i]
        m_new = jnp.maximum(m_old, jnp.max(s_t, axis=0, keepdims=True))
        alpha = jnp.exp(m_old - m_new)
        p = jnp.exp(s_t - m_new)
        l_scr[i] = alpha * l_scr[i] + jnp.sum(p, axis=0, keepdims=True)
        acc_scr[i] = alpha * acc_scr[i] + _dot(v_t, p.astype(BF16))
        m_scr[i] = m_new

    def v_chunks(ref, head, c0, n):
        return jnp.concatenate([ref[0, head, c0 + c] for c in range(n)], axis=1)

    n_ctx = kc_ref.shape[1]
    for i in range(2):
        s_t = _dot_nt(kc_ref[0, :, koffs[i]:koffs[i] + LANE], q_ref[0, i])
        update(i, s_t, v_chunks(vc_ref, vsel[i], 0, n_ctx // CHUNK))

    if mode == "window":
        n_keys = tq + 2 * WINDOW
        q0 = qi * tq
        start = pl.multiple_of(jnp.clip(q0 - WINDOW, 0, t_lat - n_keys), CHUNK)
        kpos = start + lax.broadcasted_iota(jnp.int32, (n_keys, tq), 0)
        qpos = q0 + lax.broadcasted_iota(jnp.int32, (n_keys, tq), 1)
        valid = jnp.abs(kpos - qpos) <= WINDOW
        for i in range(2):
            s_t = _dot_nt(kl_ref[0, pl.ds(start, n_keys), koffs[i]:koffs[i] + LANE], q_ref[0, i])
            s_t = jnp.where(valid, s_t, NEG)
            update(i, s_t, v_chunks(vl_ref, vsel[i], start // CHUNK, n_keys // CHUNK))
    elif mode == "nbr":
        rows = t_lat // GRID_W
        start_row = jnp.clip(2 * qi - NA_KH // 2, 0, rows - NBR_KEYS // GRID_W)
        start = pl.multiple_of(start_row * GRID_W, CHUNK)
        for i in range(2):
            s_t = _dot_nt(kl_ref[0, pl.ds(start, NBR_KEYS), koffs[i]:koffs[i] + LANE], q_ref[0, i])
            s_t = s_t + bm_ref[0, i]
            update(i, s_t, v_chunks(vl_ref, vsel[i], start // CHUNK, NBR_KEYS // CHUNK))

    o_t = jnp.concatenate([acc_scr[i] / l_scr[i] for i in range(2)], axis=0)
    o_ref[0] = o_t.T


def _flash_dense_kernel(q_ref, kc_ref, vc_ref, kl_ref, vl_ref, o_ref,
                        sa_scr, sb_scr, m_scr, l_scr, acc_scr, *, tq, t_lat, koffs, vsel, shared):
    n_blk = t_lat // TK
    cpb = TK // CHUNK
    q = [q_ref[0, g] for g in range(2)]
    q_all = jnp.concatenate(q, axis=0) if shared else None

    def scores(k_ref, r0, n):
        if shared:
            return _dot_nt(k_ref[0, pl.ds(r0, n), koffs[0]:koffs[0] + LANE], q_all)
        return jnp.concatenate(
            [_dot_nt(k_ref[0, pl.ds(r0, n), koffs[g]:koffs[g] + LANE], q[g]) for g in range(2)], axis=1)

    def values(v_ref, head, c0, n):
        return jnp.concatenate([v_ref[0, head, c0 + c] for c in range(n)], axis=1)

    def update(s_t, v_ref, c0, n):
        m_old = m_scr[...]
        m_new = jnp.maximum(m_old, jnp.max(s_t, axis=0, keepdims=True))
        alpha = jnp.exp(m_old - m_new)
        p = jnp.exp(s_t - m_new)
        l_scr[...] = alpha * l_scr[...] + jnp.sum(p, axis=0, keepdims=True)
        pb = p.astype(BF16)
        if shared:
            pv = _dot(values(v_ref, vsel[0], c0, n), pb)
        else:
            pv = jnp.concatenate([_dot(values(v_ref, vsel[g], c0, n), pb[:, g * tq:(g + 1) * tq])
                                  for g in range(2)], axis=1)
        acc_scr[...] = alpha * acc_scr[...] + pv
        m_scr[...] = m_new

    m_scr[...] = jnp.full(m_scr.shape, NEG, F32)
    l_scr[...] = jnp.zeros(l_scr.shape, F32)
    acc_scr[...] = jnp.zeros(acc_scr.shape, F32)

    n_ctx = kc_ref.shape[1]
    sa_scr[...] = scores(kl_ref, 0, TK)
    update(scores(kc_ref, 0, n_ctx), vc_ref, 0, n_ctx // CHUNK)

    def body(jj, carry):
        b0 = 2 * jj
        sb_scr[...] = scores(kl_ref, pl.multiple_of((b0 + 1) * TK, TK), TK)
        update(sa_scr[...], vl_ref, b0 * cpb, cpb)
        sa_scr[...] = scores(kl_ref, pl.multiple_of((b0 + 2) * TK, TK), TK)
        update(sb_scr[...], vl_ref, (b0 + 1) * cpb, cpb)
        return carry
    lax.fori_loop(0, n_blk // 2 - 1, body, 0)

    sb_scr[...] = scores(kl_ref, (n_blk - 1) * TK, TK)
    update(sa_scr[...], vl_ref, (n_blk - 2) * cpb, cpb)
    update(sb_scr[...], vl_ref, (n_blk - 1) * cpb, cpb)

    o_t = acc_scr[...] / l_scr[...]
    o_ref[0] = jnp.concatenate([o_t[:, :tq], o_t[:, tq:]], axis=0).T


def _flash(q, k_ctx, vt_ctx, k_lat, vt_lat, *, mode, kind, sink=None, bias=None):
    bsz, _, t_q, _ = q.shape
    n_ctx = k_ctx.shape[1]
    tq = TQ_NBR if mode == "nbr" else min(TQ, t_q)
    nq = t_q // tq
    t_lat = 0 if k_lat is None else k_lat.shape[1]
    if kind == "gqa":
        kblk, kidx, koffs, vblk, vsel = LANE, (lambda p: 0), (0, 0), 1, (0, 0)
    elif kind == "mha":
        kblk, kidx, koffs, vblk, vsel = LANE, (lambda p: p), (0, 0), 2, (0, 1)
    else:
        kblk, kidx, koffs, vblk, vsel = 2 * LANE, (lambda p: p), (0, LANE), 2, (0, 1)

    in_specs, args = [], []
    if sink is not None:
        in_specs.append(pl.BlockSpec(memory_space=pltpu.SMEM))
        args.append(sink)
    in_specs.append(pl.BlockSpec((1, 2, tq, LANE), lambda b, p, i: (b, p, i, 0)))
    args.append(q)

    def kv_specs(t_len):
        return [pl.BlockSpec((1, t_len, kblk), lambda b, p, i: (b, 0, kidx(p))),
                pl.BlockSpec((1, vblk, t_len // CHUNK, HEAD_DIM, CHUNK),
                             lambda b, p, i: (b, p, 0, 0, 0))]

    in_specs += kv_specs(n_ctx)
    args += [k_ctx, vt_ctx]
    if mode != "ctx":
        in_specs += kv_specs(t_lat)
        args += [k_lat, vt_lat]
    if mode == "nbr":
        def variant(i):
            return jnp.where(i == 0, 0, jnp.where(i == 1, 1,
                             jnp.where(i == nq - 2, 3, jnp.where(i == nq - 1, 4, 2))))
        in_specs.append(pl.BlockSpec((1, 2, NBR_KEYS, tq), lambda b, p, i: (variant(i), p, 0, 0)))
        args.append(bias)

    if mode == "dense":
        assert sink is None and (t_lat // TK) % 2 == 0
        body = functools.partial(_flash_dense_kernel, tq=tq, t_lat=t_lat, koffs=koffs, vsel=vsel,
                                 shared=kind == "gqa")
        scratch = [pltpu.VMEM((TK, 2 * tq), F32), pltpu.VMEM((TK, 2 * tq), F32),
                   pltpu.VMEM((1, 2 * tq), F32), pltpu.VMEM((1, 2 * tq), F32),
                   pltpu.VMEM((HEAD_DIM, 2 * tq), F32)]
    else:
        body = functools.partial(_flash_kernel, mode=mode, has_sink=sink is not None, tq=tq,
                                 t_lat=t_lat, koffs=koffs, vsel=vsel)
        scratch = [pltpu.VMEM((2, 1, tq), F32), pltpu.VMEM((2, 1, tq), F32),
                   pltpu.VMEM((2, HEAD_DIM, tq), F32)]
    return pl.pallas_call(
        body,
        grid=(bsz, 2, nq),
        in_specs=in_specs,
        out_specs=pl.BlockSpec((1, tq, LANE), lambda b, p, i: (b, i, p)),
        out_shape=jax.ShapeDtypeStruct((bsz, t_q, GROUP_WIDTH), F32),
        scratch_shapes=scratch,
        compiler_params=_cparams(3, VMEM_LIMIT),
        name=f"attn_{mode}_{kind}",
    )(*args)


def _merge_kernel(h_ref, oa_ref, ob_ref, oc_ref, od_ref, g_ref, w_ref, gate_ref, o_ref):
    acc = None
    for gi, ref in enumerate((oa_ref, ob_ref, oc_ref, od_ref)):
        y = ref[0]
        y = y * lax.rsqrt(jnp.mean(y * y, axis=-1, keepdims=True) + EPS) * g_ref[gi:gi + 1, :]
        part = _dot(y.astype(BF16), w_ref[gi * GROUP_WIDTH:(gi + 1) * GROUP_WIDTH, :])
        acc = part if acc is None else acc + part
    o_ref[0] = h_ref[0] + gate_ref[0] * acc


def _merge(h, outs, gain, w_out, gate, tm):
    bsz, t_len, d = h.shape
    ospec = pl.BlockSpec((1, tm, GROUP_WIDTH), lambda b, t: (b, t, 0))
    return pl.pallas_call(
        _merge_kernel,
        grid=(bsz, t_len // tm),
        in_specs=[pl.BlockSpec((1, tm, d), lambda b, t: (b, t, 0)), ospec, ospec, ospec, ospec,
                  pl.BlockSpec((N_HEADS, GROUP_WIDTH), lambda b, t: (0, 0)),
                  pl.BlockSpec((d, d), lambda b, t: (0, 0)),
                  pl.BlockSpec((1, 1, d), lambda b, t: (b, 0, 0))],
        out_specs=pl.BlockSpec((1, tm, d), lambda b, t: (b, t, 0)),
        out_shape=jax.ShapeDtypeStruct(h.shape, F32),
        compiler_params=_cparams(2, VMEM_LIMIT),
        name="merge_out",
    )(h, *outs, gain, w_out, gate)


def _norm_mod(x, g, mod_ref):
    xn = x * lax.rsqrt(jnp.mean(x * x, axis=-1, keepdims=True) + EPS) * g
    return xn * (1.0 + mod_ref[0, 1:2, :]) + mod_ref[0, 0:1, :]


def _ffn_kernel(h_ref, mod_ref, g_ref, wg_ref, wu_ref, wd_ref, o_ref, *, n_chunks, fc):
    x = h_ref[0]
    f = _norm_mod(x, g_ref[...], mod_ref).astype(BF16)
    acc = None
    for c in range(n_chunks):
        gate = _dot(f, wg_ref[:, c * fc:(c + 1) * fc])
        up = _dot(f, wu_ref[:, c * fc:(c + 1) * fc])
        act = (gate * jax.nn.sigmoid(gate) * up).astype(BF16)
        part = _dot(act, wd_ref[c * fc:(c + 1) * fc, :])
        acc = part if acc is None else acc + part
    o_ref[0] = x + mod_ref[0, 2:3, :] * acc


def _ffn(h, mod, gain, w_gate, w_up, w_down, tm):
    bsz, t_len, d = h.shape
    d_ff = w_gate.shape[1]
    n_chunks = 2
    fc = d_ff // n_chunks
    return pl.pallas_call(
        functools.partial(_ffn_kernel, n_chunks=n_chunks, fc=fc),
        grid=(bsz, t_len // tm),
        in_specs=[pl.BlockSpec((1, tm, d), lambda b, t: (b, t, 0)),
                  pl.BlockSpec((1, 3, d), lambda b, t: (b, 0, 0)),
                  pl.BlockSpec((1, d), lambda b, t: (0, 0)),
                  pl.BlockSpec((d, d_ff), lambda b, t: (0, 0), pipeline_mode=pl.Buffered(1)),
                  pl.BlockSpec((d, d_ff), lambda b, t: (0, 0), pipeline_mode=pl.Buffered(1)),
                  pl.BlockSpec((d_ff, d), lambda b, t: (0, 0), pipeline_mode=pl.Buffered(1))],
        out_specs=pl.BlockSpec((1, tm, d), lambda b, t: (b, t, 0)),
        out_shape=jax.ShapeDtypeStruct(h.shape, F32),
        compiler_params=_cparams(2, VMEM_LIMIT),
        name="ffn_dense",
    )(h, mod, gain, w_gate, w_up, w_down)


def _route_kernel(h_ref, mod_ref, g_ref, rt_ref, f_ref, e_ref, gate_ref, rank_ref, cnt_ref,
                  carry_scr, *, tm):
    step = pl.program_id(0) * pl.num_programs(1) + pl.program_id(1)

    @pl.when(step == 0)
    def _():
        carry_scr[...] = jnp.zeros_like(carry_scr)

    f = _norm_mod(h_ref[0], g_ref[...], mod_ref)
    f_ref[0] = f
    logits = _dot_nt(rt_ref[...], f.astype(BF16))
    row = lax.broadcasted_iota(jnp.int32, (N_EXPERTS, tm), 0)
    m1 = jnp.max(logits, axis=0, keepdims=True)
    e1 = jnp.min(jnp.where(logits == m1, row, N_EXPERTS), axis=0, keepdims=True)
    rest = jnp.where(row == e1, -jnp.inf, logits)
    m2 = jnp.max(rest, axis=0, keepdims=True)
    e2 = jnp.min(jnp.where(rest == m2, row, N_EXPERTS), axis=0, keepdims=True)
    z = jnp.exp(m2 - m1)
    den = 1.0 + z
    gate_ref[0, 0:1, :] = 1.0 / den
    gate_ref[0, 1:2, :] = z / den
    e_ref[0, 0:1, :] = e1
    e_ref[0, 1:2, :] = e2

    ind = jnp.logical_or(row == e1, row == e2)
    ii = lax.broadcasted_iota(jnp.int32, (tm, tm), 0)
    jj = lax.broadcasted_iota(jnp.int32, (tm, tm), 1)
    upper = jnp.where(ii < jj, 1.0, 0.0).astype(BF16)
    before = _dot(jnp.where(ind, 1.0, 0.0).astype(BF16), upper) + carry_scr[...]
    rank_ref[0, 0:1, :] = jnp.sum(jnp.where(row == e1, before, 0.0), axis=0, keepdims=True).astype(jnp.int32)
    rank_ref[0, 1:2, :] = jnp.sum(jnp.where(row == e2, before, 0.0), axis=0, keepdims=True).astype(jnp.int32)
    carry_scr[...] = carry_scr[...] + jnp.sum(jnp.where(ind, 1.0, 0.0), axis=1, keepdims=True)
    cnt_ref[...] = carry_scr[...].astype(jnp.int32)


def _route(h, mod, gain, router_t, tm):
    bsz, t_len, d = h.shape
    nt = t_len // tm
    n_tiles = bsz * nt
    small = lambda: pl.BlockSpec((1, 2, tm), lambda b, t: (b * nt + t, 0, 0))
    return pl.pallas_call(
        functools.partial(_route_kernel, tm=tm),
        grid=(bsz, nt),
        in_specs=[pl.BlockSpec((1, tm, d), lambda b, t: (b, t, 0)),
                  pl.BlockSpec((1, 3, d), lambda b, t: (b, 0, 0)),
                  pl.BlockSpec((1, d), lambda b, t: (0, 0)),
                  pl.BlockSpec((N_EXPERTS, d), lambda b, t: (0, 0))],
        out_specs=[pl.BlockSpec((1, tm, d), lambda b, t: (b, t, 0)),
                   small(), small(), small(),
                   pl.BlockSpec((N_EXPERTS, 1), lambda b, t: (0, 0))],
        out_shape=[jax.ShapeDtypeStruct(h.shape, F32),
                   jax.ShapeDtypeStruct((n_tiles, 2, tm), jnp.int32),
                   jax.ShapeDtypeStruct((n_tiles, 2, tm), F32),
                   jax.ShapeDtypeStruct((n_tiles, 2, tm), jnp.int32),
                   jax.ShapeDtypeStruct((N_EXPERTS, 1), jnp.int32)],
        scratch_shapes=[pltpu.VMEM((N_EXPERTS, 1), F32)],
        compiler_params=_cparams(2, VMEM_LIMIT),
        name="moe_route",
    )(h, mod, gain, router_t)


def _dispatch_kernel(pstart_ref, e_ref, rank_ref, f_ref, xs_in_ref, xs_ref, sem, *, tm):
    del xs_in_ref

    def row_copy(i, k):
        dest = pstart_ref[e_ref[0, k, i]] + rank_ref[0, k, i]
        return pltpu.make_async_copy(f_ref.at[pl.ds(i, 1)], xs_ref.at[pl.ds(dest, 1)], sem)

    def issue(i, carry):
        row_copy(i, 0).start()
        row_copy(i, 1).start()
        return carry
    lax.fori_loop(0, tm, issue, 0)
    for _ in range(2):
        pltpu.make_async_copy(f_ref, xs_ref.at[pl.ds(0, tm)], sem).wait()


def _dispatch(f_flat, e_idx, rank, pstart, n_rows, tm):
    n_tok, d = f_flat.shape
    xs0 = jnp.zeros((n_rows, d), F32)
    smem = lambda: pl.BlockSpec((1, 2, tm), lambda t, ps: (t, 0, 0), memory_space=pltpu.SMEM)
    return pl.pallas_call(
        functools.partial(_dispatch_kernel, tm=tm),
        grid_spec=pltpu.PrefetchScalarGridSpec(
            num_scalar_prefetch=1,
            grid=(n_tok // tm,),
            in_specs=[smem(), smem(),
                      pl.BlockSpec((tm, d), lambda t, ps: (t, 0)),
                      pl.BlockSpec(memory_space=pl.ANY)],
            out_specs=pl.BlockSpec(memory_space=pl.ANY),
            scratch_shapes=[pltpu.SemaphoreType.DMA(())]),
        out_shape=jax.ShapeDtypeStruct((n_rows, d), F32),
        input_output_aliases={4: 0},
        compiler_params=pltpu.CompilerParams(dimension_semantics=("arbitrary",),
                                             has_side_effects=True),
        name="moe_dispatch",
    )(pstart, e_idx, rank, f_flat, xs0)


def _moe_kernel(be_ref, nrow_ref, nused_ref, x_ref, wg_ref, wu_ref, wd_ref, y_ref, *, tb, sub):
    del be_ref, nused_ref
    blk = pl.program_id(0)
    fi = pl.program_id(1)
    n_valid = nrow_ref[blk]

    for s in range(tb // sub):
        @pl.when(n_valid > s * sub)
        def _(s=s):
            x = x_ref[s * sub:(s + 1) * sub, :].astype(BF16)
            gate = _dot(x, wg_ref[0].astype(BF16))
            up = _dot(x, wu_ref[0].astype(BF16))
            act = (gate * jax.nn.sigmoid(gate) * up).astype(BF16)
            part = _dot(act, wd_ref[0].astype(BF16))

            @pl.when(fi == 0)
            def _():
                y_ref[s * sub:(s + 1) * sub, :] = part

            @pl.when(fi != 0)
            def _():
                y_ref[s * sub:(s + 1) * sub, :] += part

        @pl.when(jnp.logical_and(n_valid <= s * sub, fi == 0))
        def _(s=s):
            y_ref[s * sub:(s + 1) * sub, :] = jnp.zeros((sub, y_ref.shape[1]), F32)


def _moe_blocks(xs, block_e, block_rows, n_used, w_gate, w_up, w_down):
    n_rows, d = xs.shape
    nb = n_rows // MOE_TB
    d_ff = w_gate.shape[2]
    nf = d_ff // MOE_TF

    def bidx(b, be, nr, nu):
        return jnp.minimum(b, nu[0] - 1)

    def fidx(b, f, nu):
        return jnp.where(b < nu[0], f, nf - 1)

    return pl.pallas_call(
        functools.partial(_moe_kernel, tb=MOE_TB, sub=MOE_TB // 2),
        grid_spec=pltpu.PrefetchScalarGridSpec(
            num_scalar_prefetch=3,
            grid=(nb, nf),
            in_specs=[pl.BlockSpec((MOE_TB, d), lambda b, f, be, nr, nu: (bidx(b, be, nr, nu), 0)),
                      pl.BlockSpec((1, d, MOE_TF),
                                   lambda b, f, be, nr, nu: (be[bidx(b, be, nr, nu)], 0, fidx(b, f, nu))),
                      pl.BlockSpec((1, d, MOE_TF),
                                   lambda b, f, be, nr, nu: (be[bidx(b, be, nr, nu)], 0, fidx(b, f, nu))),
                      pl.BlockSpec((1, MOE_TF, d),
                                   lambda b, f, be, nr, nu: (be[bidx(b, be, nr, nu)], fidx(b, f, nu), 0))],
            out_specs=pl.BlockSpec((MOE_TB, d), lambda b, f, be, nr, nu: (b, 0))),
        out_shape=jax.ShapeDtypeStruct((n_rows, d), F32),
        compiler_params=_cparams(2, VMEM_LIMIT),
        name="moe_experts",
    )(block_e, block_rows, n_used, xs, w_gate, w_up, w_down)


def _combine_kernel(pstart_ref, e_ref, rank_ref, h_ref, g2_ref, gates_ref, y_ref, o_ref,
                    buf0, buf1, sem, *, tm):
    def row_copy(i, k, buf):
        src = pstart_ref[e_ref[0, k, i]] + rank_ref[0, k, i]
        return pltpu.make_async_copy(y_ref.at[pl.ds(src, 1)], buf.at[pl.ds(i, 1)], sem)

    def issue(i, carry):
        row_copy(i, 0, buf0).start()
        row_copy(i, 1, buf1).start()
        return carry
    lax.fori_loop(0, tm, issue, 0)
    pltpu.make_async_copy(y_ref.at[pl.ds(0, tm)], buf0, sem).wait()
    pltpu.make_async_copy(y_ref.at[pl.ds(0, tm)], buf1, sem).wait()

    gates = gates_ref[...]
    moe = buf0[...] * gates[:, 0:1] + buf1[...] * gates[:, 1:2]
    o_ref[0] = h_ref[0] + g2_ref[0] * moe


def _combine(h, gate2, gates_col, y, e_idx, rank, pstart, tm):
    bsz, t_len, d = h.shape
    nt = t_len // tm
    smem = lambda: pl.BlockSpec((1, 2, tm), lambda b, t, ps: (b * nt + t, 0, 0),
                                memory_space=pltpu.SMEM)
    return pl.pallas_call(
        functools.partial(_combine_kernel, tm=tm),
        grid_spec=pltpu.PrefetchScalarGridSpec(
            num_scalar_prefetch=1,
            grid=(bsz, nt),
            in_specs=[smem(), smem(),
                      pl.BlockSpec((1, tm, d), lambda b, t, ps: (b, t, 0)),
                      pl.BlockSpec((1, 1, d), lambda b, t, ps: (b, 0, 0)),
                      pl.BlockSpec((tm, 2), lambda b, t, ps: (b * nt + t, 0)),
                      pl.BlockSpec(memory_space=pl.ANY)],
            out_specs=pl.BlockSpec((1, tm, d), lambda b, t, ps: (b, t, 0)),
            scratch_shapes=[pltpu.VMEM((tm, d), F32), pltpu.VMEM((tm, d), F32),
                            pltpu.SemaphoreType.DMA(())]),
        out_shape=jax.ShapeDtypeStruct(h.shape, F32),
        compiler_params=_cparams(2, VMEM_LIMIT),
        name="moe_combine",
    )(pstart, e_idx, rank, h, gate2, gates_col, y)


def _moe(h, mod, gain, router, w_gate, w_up, w_down):
    bsz, t_len, d = h.shape
    n_tok = bsz * t_len
    tm = min(TM_LAT, t_len)
    f, e_idx, gates, rank, counts = _route(h, mod, gain, router.T.astype(BF16), tm)

    counts = counts[:, 0]
    padded = (counts + MOE_TB - 1) // MOE_TB * MOE_TB
    pends = jnp.cumsum(padded)
    pstart = (pends - padded).astype(jnp.int32)
    nb = (2 * n_tok + N_EXPERTS * (MOE_TB - 1) + MOE_TB - 1) // MOE_TB
    bstart = jnp.arange(nb, dtype=jnp.int32) * MOE_TB
    block_e = jnp.minimum(jnp.sum(bstart[:, None] >= pends[None, :], axis=1), N_EXPERTS - 1).astype(jnp.int32)
    block_rows = jnp.clip(pstart[block_e] + counts[block_e] - bstart, 0, MOE_TB).astype(jnp.int32)
    n_used = (pends[-1] // MOE_TB).astype(jnp.int32).reshape(1)

    xs = _dispatch(f.reshape(n_tok, d), e_idx, rank, pstart, nb * MOE_TB, tm)
    y = _moe_blocks(xs, block_e, block_rows, n_used, w_gate, w_up, w_down)
    gates_col = gates.transpose(0, 2, 1).reshape(n_tok, 2)
    return _combine(h, mod[:, 2:3, :], gates_col, y, e_idx, rank, pstart, tm)


def _rope_tables(n_tokens, rot_dim):
    t = jnp.arange(n_tokens)
    row = (t // GRID_W).astype(F32)[:, None]
    col = (t % GRID_W).astype(F32)[:, None]
    half = rot_dim // 2
    inv_freq = ROPE_THETA ** (-jnp.arange(0, half, 2, dtype=F32) / half)
    ang = jnp.concatenate([row * inv_freq, row * inv_freq, col * inv_freq, col * inv_freq], axis=-1)
    return jnp.cos(ang), jnp.sin(ang)


def _rope_inputs(n_tokens):
    cos64, sin64 = _rope_tables(n_tokens, HEAD_DIM)
    lane = np.arange(HEAD_DIM)
    upper = jnp.asarray((lane % 32) >= 16)
    cos = jnp.tile(cos64, (1, 2))
    sin_a = jnp.tile(jnp.where(upper, sin64, 0.0), (1, 2))
    sin_b = jnp.tile(jnp.where(upper, 0.0, -sin64), (1, 2))

    cos32, sin32 = _rope_tables(n_tokens, MLA_ROPE)
    lane = np.arange(MLA_ROPE)
    upper = jnp.asarray((lane % 16) >= 8)
    ones = jnp.ones((n_tokens, MLA_NOPE), F32)
    zeros = jnp.zeros((n_tokens, MLA_NOPE), F32)
    pad1 = jnp.ones((n_tokens, LANE - MLA_QK), F32)
    pad0 = jnp.zeros((n_tokens, LANE - MLA_QK), F32)
    cos_m = jnp.concatenate([ones, cos32, pad1], axis=1)
    sin_am = jnp.concatenate([zeros, jnp.where(upper, sin32, 0.0), pad0], axis=1)
    sin_bm = jnp.concatenate([zeros, jnp.where(upper, 0.0, -sin32), pad0], axis=1)
    return cos, sin_a, sin_b, cos_m, sin_am, sin_bm


def _block_diag_ones(width, block):
    idx = np.arange(width) // block
    return jnp.asarray(idx[:, None] == idx[None, :], dtype=BF16)


def _layer_weights(l, norm1_g, w_in, a_qn, a_kn, b_qn, b_kn, c_qn, c_kn, d_q_norm, d_w_uq,
                   d_kv_norm, d_w_ukv, d_qn, d_kn):
    w = w_in[l]
    d = w.shape[0]

    def cols(a, b):
        return w[:, a:b]

    def perm_q(a):
        q = cols(a, a + 256).reshape(d, 4, HEAD_DIM)
        return q[:, jnp.array([0, 2, 1, 3]), :].reshape(d, 256)

    z = lambda n: jnp.zeros((d, n), w.dtype)
    w_new = jnp.concatenate([
        perm_q(0), perm_q(512), cols(1024, 1280), cols(1280, 1536),
        cols(256, 384), cols(768, 896),
        cols(384, 512), cols(896, 1024), cols(1536, 1792),
        cols(1792, 1984), z(64),
        cols(1984, 2112),
        z(MLA_NOPE), cols(2112, 2144), z(LANE - MLA_QK),
    ], axis=1).astype(BF16)

    t4 = lambda g: jnp.tile(g[l], 4)
    gq = jnp.stack([t4(a_qn), t4(b_qn), t4(c_qn)])
    gk = jnp.stack([t4(c_kn), jnp.concatenate([jnp.tile(a_kn[l], 2), jnp.tile(b_kn[l], 2)])])
    pad96 = lambda g: jnp.tile(jnp.concatenate([g[l], jnp.zeros((LANE - MLA_QK,), F32)]), 2)[None, :]
    gdq = jnp.concatenate([d_q_norm[l], jnp.zeros((256 - MLA_Q_RANK,), F32)])[None, :]

    wuq = d_w_uq[l].reshape(MLA_Q_RANK, N_HEADS, MLA_QK)
    wuq = jnp.pad(wuq, ((0, 256 - MLA_Q_RANK), (0, 0), (0, LANE - MLA_QK))).reshape(256, N_HEADS * LANE)
    wukv = d_w_ukv[l].reshape(MLA_KV_RANK, N_HEADS, MLA_NOPE + HEAD_DIM)
    wk = jnp.pad(wukv[:, :, :MLA_NOPE], ((0, 0), (0, 0), (0, LANE - MLA_NOPE))).reshape(MLA_KV_RANK, N_HEADS * LANE)
    wv = wukv[:, :, MLA_NOPE:].reshape(MLA_KV_RANK, N_HEADS * HEAD_DIM)
    return dict(
        g1=norm1_g[l][None, :], w_in=w_new, gq=gq, gk=gk, gdq=gdq, gdkv=d_kv_norm[l][None, :],
        gdqn=pad96(d_qn), gdkn=pad96(d_kn),
        w_uq=wuq.astype(BF16), w_ukv=jnp.concatenate([wk, wv], axis=1).astype(BF16),
        m64=_block_diag_ones(256, HEAD_DIM), m128=_block_diag_ones(256, LANE))


N_DR = 2 * NA_KH - 1
N_DC = 2 * NA_KW - 1


def _toeplitz_kernel(rpb_ref, onehot_ref, o_ref):
    r = rpb_ref[...]
    hi = r.astype(BF16)
    r1 = r - hi.astype(F32)
    mid = r1.astype(BF16)
    lo = (r1 - mid.astype(F32)).astype(BF16)
    e = onehot_ref[...]
    o_ref[...] = _dot(hi, e) + _dot(mid, e) + _dot(lo, e)


def _nbr_bias(rpb, rows):
    n_heads = rpb.shape[0]
    m = np.arange(32)[:, None, None]
    kc = np.arange(GRID_W)[None, :, None]
    qc = np.arange(GRID_W)[None, None, :]
    onehot = jnp.asarray((m == kc - qc + NA_KW - 1).reshape(32, GRID_W * GRID_W), dtype=BF16)
    rpb_p = jnp.pad(rpb.reshape(n_heads * N_DR, N_DC), ((0, 64 - n_heads * N_DR), (0, 32 - N_DC)))
    toe = pl.pallas_call(
        _toeplitz_kernel,
        out_shape=jax.ShapeDtypeStruct((64, GRID_W * GRID_W), F32),
        name="nbr_bias_toeplitz",
    )(rpb_p, onehot)
    toe = toe[:n_heads * N_DR].reshape(n_heads, N_DR, GRID_W, GRID_W)

    n_krow = NBR_KEYS // GRID_W
    toe_p = jnp.pad(toe, ((0, 0), (n_krow, n_krow), (0, 0), (0, 0)))
    cs = np.clip(np.arange(GRID_W) - NA_KW // 2, 0, GRID_W - NA_KW)
    kcol = np.arange(GRID_W)[:, None]
    col_ok = (kcol >= cs[None, :]) & (kcol < cs[None, :] + NA_KW)
    variants = []
    for r in (0, 2, 4, rows - 4, rows - 2):
        start_row = min(max(r - NA_KH // 2, 0), rows - n_krow)
        per_q = []
        for q in range(2):
            qr = r + q
            rs = min(max(qr - NA_KH // 2, 0), rows - NA_KH)
            lo = start_row - qr + NA_KH - 1 + n_krow
            slab = toe_p[:, lo:lo + n_krow]
            kr = start_row + np.arange(n_krow)
            row_ok = (kr >= rs) & (kr < rs + NA_KH)
            ok = row_ok[:, None, None] & col_ok[None, :, :]
            per_q.append(jnp.where(jnp.asarray(ok)[None], slab, NEG))
        variants.append(jnp.stack(per_q, axis=3))
    return jnp.stack(variants, axis=0).reshape(5, n_heads, NBR_KEYS, TQ_NBR)


def _mixers(h_lat, h_ctx, mod_lat, mod_ctx, lw, rope_tabs, a_sink, nbr_bias, with_ctx_out):
    lat = _proj(h_lat, mod_lat, lw, rope_tabs, TM_LAT)
    cx = _proj(h_ctx, mod_ctx, lw, None, h_ctx.shape[1])

    def attend(q, name, kind, mode, **kw):
        if mode == "ctx":
            return _flash(q[name + "_q"], cx[name + "_k"], cx[name + "_vt"], None, None,
                          mode=mode, kind=kind, **kw)
        return _flash(q[name + "_q"], cx[name + "_k"], cx[name + "_vt"], lat[name + "_k"],
                      lat[name + "_vt"], mode=mode, kind=kind, **kw)

    outs_lat = (attend(lat, "a", "gqa", "window", sink=a_sink),
                attend(lat, "b", "gqa", "dense"),
                attend(lat, "c", "mha", "nbr", bias=nbr_bias),
                attend(lat, "d", "mla", "dense"))
    outs_ctx = None
    if with_ctx_out:
        outs_ctx = (attend(cx, "a", "gqa", "ctx", sink=a_sink),
                    attend(cx, "b", "gqa", "ctx"),
                    attend(cx, "c", "mha", "ctx"),
                    attend(cx, "d", "mla", "ctx"))
    return outs_lat, outs_ctx


def kernel(x, c, ctx, c_ctx, ada_w, ada_b, norm1_g, norm2_g, w_in, a_qn, a_kn, a_sink, b_qn, b_kn,
           c_qn, c_kn, c_rpb, d_q_norm, d_w_uq, d_kv_norm, d_w_ukv, d_qn, d_kn, out_norm_g, w_out,
           ffn_w_gate, ffn_w_up, ffn_w_down, moe_router, moe_w_gate, moe_w_up, moe_w_down):
    bsz, seq, d = x.shape
    depth = ada_w.shape[0]
    n_ctx = ctx.shape[1]
    assert d == D_MODEL and seq % TM_LAT == 0 and seq >= NBR_KEYS and n_ctx % TQ == 0
    assert (seq // GRID_W) % 2 == 0 and seq // TQ_NBR >= 5

    cvec = jnp.concatenate([c, c_ctx[None, :], jnp.zeros((8 - bsz - 1, d), F32)], axis=0)
    mod_all = _adaln(cvec, ada_w, ada_b)
    rope_tabs = _rope_inputs(seq)

    h_lat, h_ctx = x, ctx
    for layer in range(depth):
        last = layer == depth - 1
        mod = mod_all[layer].reshape(8, 6, d)
        mod_lat = mod[:bsz]
        mod_ctx = jnp.broadcast_to(mod[bsz:bsz + 1], (bsz, 6, d))
        lw = _layer_weights(layer, norm1_g, w_in, a_qn, a_kn, b_qn, b_kn, c_qn, c_kn, d_q_norm,
                            d_w_uq, d_kv_norm, d_w_ukv, d_qn, d_kn)
        nbr_bias = _nbr_bias(c_rpb[layer], seq // GRID_W)
        outs_lat, outs_ctx = _mixers(h_lat, h_ctx, mod_lat[:, 0:2], mod_ctx[:, 0:2], lw, rope_tabs,
                                     a_sink[layer], nbr_bias, not last)
        gain_o = out_norm_g[layer].reshape(N_HEADS, GROUP_WIDTH)
        w_o = w_out[layer].astype(BF16)
        h_lat = _merge(h_lat, outs_lat, gain_o, w_o, mod_lat[:, 2:3], TM_LAT)
        if not last:
            h_ctx = _merge(h_ctx, outs_ctx, gain_o, w_o, mod_ctx[:, 2:3], n_ctx)

        j = layer // 2
        g2 = norm2_g[layer][None, :]
        if layer % 2 == 0:
            wg, wu, wd = (ffn_w_gate[j].astype(BF16), ffn_w_up[j].astype(BF16),
                          ffn_w_down[j].astype(BF16))
            h_lat = _ffn(h_lat, mod_lat[:, 3:6], g2, wg, wu, wd, TM_LAT)
            if not last:
                h_ctx = _ffn(h_ctx, mod_ctx[:, 3:6], g2, wg, wu, wd, n_ctx)
        else:
            h_lat = _moe(h_lat, mod_lat[:, 3:6], g2, moe_router[j], moe_w_gate[j], moe_w_up[j],
                         moe_w_down[j])
            if not last:
                h_ctx = _moe(h_ctx, mod_ctx[:, 3:6], g2, moe_router[j], moe_w_gate[j], moe_w_up[j],
                             moe_w_down[j])
    return h_lat
```

```python
import functools

import numpy as np
import jax
import jax.numpy as jnp
from jax import lax
from jax.experimental import pallas as pl
from jax.experimental.pallas import tpu as pltpu

F32 = jnp.float32
BF16 = jnp.bfloat16

D_MODEL = 1024
GRID_W = 64
HEAD_DIM = 64
N_HEADS = 4
GROUP_WIDTH = 256
WINDOW = 128
NA_KH = 8
NA_KW = 16
MLA_NOPE = 64
MLA_ROPE = 32
MLA_QK = MLA_NOPE + MLA_ROPE
MLA_Q_RANK = 192
MLA_KV_RANK = 128
ROPE_THETA = 10000.0
EPS = 1e-6
N_EXPERTS = 8
LOG2E = 1.4426950408889634
Q_SCALE_HEAD = HEAD_DIM ** -0.5 * LOG2E
Q_SCALE_MLA = MLA_QK ** -0.5 * LOG2E
NEG = -1e30

LANE = 128
CHUNK = 128
PROJ_W = 2304

C_AQ, C_BQ, C_CQ, C_CK, C_AK, C_BK = 0, 256, 512, 768, 1024, 1152
C_V = 1280
C_DCQ, C_DCKV, C_KR = 1792, 2048, 2176

TM_LAT = 512
TQ = 256
TQ_NBR = 128
NBR_KEYS = 640
MOE_TB = 1024
MOE_TF = 512
VMEM_LIMIT = 56 * 1024 * 1024


def _cparams(n_axes, vmem=None):
    return pltpu.CompilerParams(dimension_semantics=("arbitrary",) * n_axes,
                                vmem_limit_bytes=vmem)


def _dot(a, b):
    return jnp.dot(a, b, preferred_element_type=F32)


def _dot_nt(a, b):
    return lax.dot_general(a, b, (((1,), (1,)), ((), ())), preferred_element_type=F32)


def _adaln_kernel(c_ref, w_ref, b_ref, o_ref):
    c = c_ref[...]
    s = c * jax.nn.sigmoid(c)
    o_ref[0] = _dot(s.astype(BF16), w_ref[0].astype(BF16)) + b_ref[0]


def _adaln(cvec, ada_w, ada_b):
    n_layers, d, n = ada_w.shape
    tn = 1536
    return pl.pallas_call(
        _adaln_kernel,
        grid=(n_layers, n // tn),
        in_specs=[pl.BlockSpec((8, d), lambda l, j: (0, 0)),
                  pl.BlockSpec((1, d, tn), lambda l, j: (l, 0, j)),
                  pl.BlockSpec((1, 1, tn), lambda l, j: (l, 0, j))],
        out_specs=pl.BlockSpec((1, 8, tn), lambda l, j: (l, 0, j)),
        out_shape=jax.ShapeDtypeStruct((n_layers, 8, n), F32),
        compiler_params=_cparams(2),
        name="adaln",
    )(cvec, ada_w, ada_b.reshape(n_layers, 1, n))


def _group_sumsq(x, m_ref):
    x2 = x * x
    hi = x2.astype(BF16)
    lo = (x2 - hi.astype(F32)).astype(BF16)
    m = m_ref[...]
    return _dot(hi, m) + _dot(lo, m)


def _rope(t, cos, sin_a, sin_b, shift):
    w = t.shape[1]
    return t * cos + pltpu.roll(t, shift, 1) * sin_a + pltpu.roll(t, w - shift, 1) * sin_b


def _tile_lanes(x, n):
    return x if n == 1 else jnp.concatenate([x] * n, axis=1)


def _proj_kernel(h_ref, mod_ref, g_ref, win_ref, gq_ref, gk_ref, gdq_ref, gdkv_ref,
                 gdqn_ref, gdkn_ref, wuq_ref, wukv_ref, m64_ref, m128_ref, *rest,
                 use_rope, tm):
    if use_rope:
        (cos_ref, sa_ref, sb_ref, cosm_ref, sam_ref, sbm_ref,
         aq_ref, bq_ref, cq_ref, dq_ref, ak_ref, bk_ref, ck_ref, dk_ref,
         avt_ref, bvt_ref, cvt_ref, dvt_ref) = rest
    else:
        (aq_ref, bq_ref, cq_ref, dq_ref, ak_ref, bk_ref, ck_ref, dk_ref,
         avt_ref, bvt_ref, cvt_ref, dvt_ref) = rest

    x = h_ref[0]
    ms = jnp.mean(x * x, axis=-1, keepdims=True)
    xn = x * lax.rsqrt(ms + EPS) * g_ref[...]
    hm = xn * (1.0 + mod_ref[0, 1:2, :]) + mod_ref[0, 0:1, :]
    pr = _dot(hm.astype(BF16), win_ref[...])

    lane = lax.broadcasted_iota(jnp.int32, (tm, LANE), 1)
    half0 = lane < HEAD_DIM

    if use_rope:
        cos2 = _tile_lanes(cos_ref[...], 2)
        sa2 = _tile_lanes(sa_ref[...], 2)
        sb2 = _tile_lanes(sb_ref[...], 2)

    def head64(col, gain, rope):
        t = pr[:, col:col + 256]
        ss = _group_sumsq(t, m64_ref)
        t = t * lax.rsqrt(ss * (1.0 / HEAD_DIM) + EPS) * gain
        if rope and use_rope:
            t = _rope(t, cos2, sa2, sb2, 16)
        return t

    gq = gq_ref[...]
    gk = gk_ref[...]

    def store_q(ref, t, placement):
        for h in range(N_HEADS):
            blk, half = placement[h]
            tb = t[:, blk * LANE:(blk + 1) * LANE]
            keep = half0 if half == 0 else jnp.logical_not(half0)
            ref[0, h] = jnp.where(keep, tb * Q_SCALE_HEAD, 0.0).astype(BF16)

    gqa_place = ((0, 0), (1, 0), (0, 1), (1, 1))
    mha_place = ((0, 0), (0, 1), (1, 0), (1, 1))
    store_q(aq_ref, head64(C_AQ, gq[0:1], True), gqa_place)
    store_q(bq_ref, head64(C_BQ, gq[1:2], True), gqa_place)
    store_q(cq_ref, head64(C_CQ, gq[2:3], False), mha_place)
    ck_ref[0] = head64(C_CK, gk[0:1], False).astype(BF16)
    kab = head64(C_AK, gk[1:2], True)
    ak_ref[0] = kab[:, :LANE].astype(BF16)
    bk_ref[0] = kab[:, LANE:].astype(BF16)

    def store_vt(ref, vt, n_heads):
        for h in range(n_heads):
            for c in range(tm // CHUNK):
                ref[0, h, c] = vt[h * HEAD_DIM:(h + 1) * HEAD_DIM,
                                  c * CHUNK:(c + 1) * CHUNK].astype(BF16)

    vt = pr[:, C_V:C_V + 512].T
    store_vt(avt_ref, vt[0:128], 2)
    store_vt(bvt_ref, vt[128:256], 2)
    store_vt(cvt_ref, vt[256:512], 4)

    if use_rope:
        cosm = _tile_lanes(cosm_ref[...], 2)
        sam = _tile_lanes(sam_ref[...], 2)
        sbm = _tile_lanes(sbm_ref[...], 2)

    def head96(t, gain):
        ss = _group_sumsq(t, m128_ref)
        t = t * lax.rsqrt(ss * (1.0 / MLA_QK) + EPS) * gain
        if use_rope:
            t = _rope(t, cosm, sam, sbm, 8)
        return t

    cq = pr[:, C_DCQ:C_DCQ + 256]
    cq = cq * lax.rsqrt(jnp.sum(cq * cq, axis=-1, keepdims=True) * (1.0 / MLA_Q_RANK) + EPS)
    qm = _dot((cq * gdq_ref[...]).astype(BF16), wuq_ref[...])
    gdqn = gdqn_ref[...]
    for pair in range(2):
        t = head96(qm[:, pair * 256:(pair + 1) * 256], gdqn)
        for i in range(2):
            dq_ref[0, 2 * pair + i] = (t[:, i * LANE:(i + 1) * LANE] * Q_SCALE_MLA).astype(BF16)

    ckv = pr[:, C_DCKV:C_DCKV + LANE]
    ckv = ckv * lax.rsqrt(jnp.mean(ckv * ckv, axis=-1, keepdims=True) + EPS)
    kv = _dot((ckv * gdkv_ref[...]).astype(BF16), wukv_ref[...])
    kr2 = _tile_lanes(pr[:, C_KR:C_KR + LANE], 2)
    gdkn = gdkn_ref[...]
    for pair in range(2):
        t = head96(kv[:, pair * 256:(pair + 1) * 256] + kr2, gdkn)
        dk_ref[0, :, pair * 256:(pair + 1) * 256] = t.astype(BF16)
    store_vt(dvt_ref, kv[:, 512:768].T, 4)


def _proj(h, mod, layer_w, rope_tabs, tm):
    bsz, t_len, d = h.shape
    use_rope = rope_tabs is not None
    nt = t_len // tm
    nch = t_len // CHUNK

    def full(a):
        nd = a.ndim
        return pl.BlockSpec(a.shape, lambda b, t, nd=nd: (0,) * nd)

    consts = [layer_w[k] for k in ("g1", "w_in", "gq", "gk", "gdq", "gdkv", "gdqn", "gdkn",
                                   "w_uq", "w_ukv", "m64", "m128")]
    in_specs = [pl.BlockSpec((1, tm, d), lambda b, t: (b, t, 0)),
                pl.BlockSpec((1, 2, d), lambda b, t: (b, 0, 0))]
    in_specs += [full(a) for a in consts]
    args = [h, mod] + consts
    if use_rope:
        in_specs += [pl.BlockSpec((tm, LANE), lambda b, t: (t, 0))] * 6
        args += list(rope_tabs)

    def qspec():
        return pl.BlockSpec((1, N_HEADS, tm, LANE), lambda b, t: (b, 0, t, 0))

    def kspec(w):
        return pl.BlockSpec((1, tm, w), lambda b, t: (b, t, 0))

    def vspec(nh):
        return pl.BlockSpec((1, nh, tm // CHUNK, HEAD_DIM, CHUNK), lambda b, t: (b, 0, t, 0, 0))

    def qshape():
        return jax.ShapeDtypeStruct((bsz, N_HEADS, t_len, LANE), BF16)

    def kshape(w):
        return jax.ShapeDtypeStruct((bsz, t_len, w), BF16)

    def vshape(nh):
        return jax.ShapeDtypeStruct((bsz, nh, nch, HEAD_DIM, CHUNK), BF16)

    names = ("a_q", "b_q", "c_q", "d_q", "a_k", "b_k", "c_k", "d_k", "a_vt", "b_vt", "c_vt", "d_vt")
    out_specs = [qspec(), qspec(), qspec(), qspec(),
                 kspec(128), kspec(128), kspec(256), kspec(512),
                 vspec(2), vspec(2), vspec(4), vspec(4)]
    out_shape = [qshape(), qshape(), qshape(), qshape(),
                 kshape(128), kshape(128), kshape(256), kshape(512),
                 vshape(2), vshape(2), vshape(4), vshape(4)]
    outs = pl.pallas_call(
        functools.partial(_proj_kernel, use_rope=use_rope, tm=tm),
        grid=(bsz, nt),
        in_specs=in_specs,
        out_specs=out_specs,
        out_shape=out_shape,
        compiler_params=_cparams(2, VMEM_LIMIT),
        name="proj_lat" if use_rope else "proj_ctx",
    )(*args)
    return dict(zip(names, outs))


def _flash_kernel(*refs, mode, has_sink, tq, t_lat, koffs, vsel, shared):
    refs = list(refs)
    sink_ref = refs.pop(0) if has_sink else None
    q_ref, kc_ref, vc_ref = refs[0], refs[1], refs[2]
    refs = refs[3:]
    kl_ref = vl_ref = bm_ref = None
    if mode != "ctx":
        kl_ref, vl_ref = refs[0], refs[1]
        refs = refs[2:]
    if mode == "nbr":
        bm_ref = refs[0]
        refs = refs[1:]
    (o_ref,) = refs

    pair = pl.program_id(1)
    qi = pl.program_id(2)
    q = [q_ref[0, g] for g in range(2)]
    q_all = jnp.concatenate(q, axis=0) if shared else None

    def scores(k_ref, r0, n):
        if shared:
            return _dot_nt(k_ref[0, pl.ds(r0, n), koffs[0]:koffs[0] + LANE], q_all)
        return jnp.concatenate(
            [_dot_nt(k_ref[0, pl.ds(r0, n), koffs[g]:koffs[g] + LANE], q[g]) for g in range(2)], axis=1)

    def v_chunks(ref, head, c0, n):
        return [ref[0, head, c0 + c] for c in range(n)]

    n_ctx = kc_ref.shape[1]
    s_parts = [scores(kc_ref, 0, n_ctx)]
    v_parts = [v_chunks(vc_ref, vsel[g], 0, n_ctx // CHUNK) for g in range(2)]
    if mode == "window":
        n_keys = tq + 2 * WINDOW
        q0 = qi * tq
        start = pl.multiple_of(jnp.clip(q0 - WINDOW, 0, t_lat - n_keys), CHUNK)
        kpos = start + lax.broadcasted_iota(jnp.int32, (n_keys, 2 * tq), 0)
        col = lax.broadcasted_iota(jnp.int32, (n_keys, 2 * tq), 1)
        qpos = q0 + jnp.where(col < tq, col, col - tq)
        s_parts.append(jnp.where(jnp.abs(kpos - qpos) <= WINDOW, scores(kl_ref, start, n_keys), NEG))
    elif mode == "nbr":
        n_keys = NBR_KEYS
        rows = t_lat // GRID_W
        start_row = jnp.clip(2 * qi - NA_KH // 2, 0, rows - NBR_KEYS // GRID_W)
        start = pl.multiple_of(start_row * GRID_W, CHUNK)
        s_parts.append(scores(kl_ref, start, n_keys)
                       + jnp.concatenate([bm_ref[0, 0], bm_ref[0, 1]], axis=1))
    if mode != "ctx":
        for g in range(2):
            v_parts[g] += v_chunks(vl_ref, vsel[g], start // CHUNK, n_keys // CHUNK)

    s_t = jnp.concatenate(s_parts, axis=0) if len(s_parts) > 1 else s_parts[0]
    m = jnp.max(s_t, axis=0, keepdims=True)
    if has_sink:
        col1 = lax.broadcasted_iota(jnp.int32, (1, 2 * tq), 1)
        sink = jnp.where(col1 < tq, sink_ref[2 * pair], sink_ref[2 * pair + 1]) * LOG2E
        m = jnp.maximum(m, sink)
    p = jnp.exp2(s_t - m)
    l = jnp.sum(p, axis=0, keepdims=True)
    if has_sink:
        l = l + jnp.exp2(sink - m)
    pb = p.astype(BF16)
    if shared:
        pv = _dot(jnp.concatenate(v_parts[0], axis=1), pb)
    else:
        pv = jnp.concatenate([_dot(jnp.concatenate(v_parts[g], axis=1), pb[:, g * tq:(g + 1) * tq])
                              for g in range(2)], axis=1)
    o_t = pv / l
    o_ref[0] = jnp.concatenate([o_t[:, :tq], o_t[:, tq:]], axis=0).T


def _flash_dense_kernel(q_ref, k_ref, v_ref, o_ref, sa_scr, sb_scr, m_scr, l_scr, acc_scr,
                        *, tq, tk, n_keys, koffs, vsel, shared):
    n_blk = n_keys // tk
    cpb = tk // CHUNK
    q = [q_ref[0, g] for g in range(2)]
    q_all = jnp.concatenate(q, axis=0) if shared else None

    def scores(blk):
        r0 = blk * tk
        if shared:
            return _dot_nt(k_ref[0, pl.ds(r0, tk), koffs[0]:koffs[0] + LANE], q_all)
        return jnp.concatenate(
            [_dot_nt(k_ref[0, pl.ds(r0, tk), koffs[g]:koffs[g] + LANE], q[g]) for g in range(2)], axis=1)

    def values(head, blk):
        return jnp.concatenate([v_ref[0, head, blk * cpb + c] for c in range(cpb)], axis=1)

    def update(s_t, blk):
        m_old = m_scr[...]
        m_new = jnp.maximum(m_old, jnp.max(s_t, axis=0, keepdims=True))
        alpha = jnp.exp2(m_old - m_new)
        p = jnp.exp2(s_t - m_new)
        l_scr[...] = alpha * l_scr[...] + jnp.sum(p, axis=0, keepdims=True)
        pb = p.astype(BF16)
        if shared:
            pv = _dot(values(vsel[0], blk), pb)
        else:
            pv = jnp.concatenate([_dot(values(vsel[g], blk), pb[:, g * tq:(g + 1) * tq])
                                  for g in range(2)], axis=1)
        acc_scr[...] = alpha * acc_scr[...] + pv
        m_scr[...] = m_new

    m_scr[...] = jnp.full(m_scr.shape, NEG, F32)
    l_scr[...] = jnp.zeros(l_scr.shape, F32)
    acc_scr[...] = jnp.zeros(acc_scr.shape, F32)
    bufs = (sa_scr, sb_scr)
    sa_scr[...] = scores(0)
    for blk in range(n_blk):
        if blk + 1 < n_blk:
            bufs[(blk + 1) % 2][...] = scores(blk + 1)
        update(bufs[blk % 2][...], blk)

    o_t = acc_scr[...] / l_scr[...]
    o_ref[0] = jnp.concatenate([o_t[:, :tq], o_t[:, tq:]], axis=0).T


def _dense_key_block(n_keys):
    for tk in (512, 384, 256, 128):
        if n_keys % tk == 0:
            return tk
    raise ValueError(f"{n_keys} keys are not a multiple of {CHUNK}")


def _flash_dense(q, k_all, vt_all, kind):
    bsz, _, t_q, _ = q.shape
    n_keys = k_all.shape[1]
    tq = min(TQ, t_q)
    tk = _dense_key_block(n_keys)
    if kind == "gqa":
        kblk, kidx, koffs, vblk, vsel = LANE, (lambda p: 0), (0, 0), 1, (0, 0)
    else:
        kblk, kidx, koffs, vblk, vsel = 2 * LANE, (lambda p: p), (0, LANE), 2, (0, 1)
    return pl.pallas_call(
        functools.partial(_flash_dense_kernel, tq=tq, tk=tk, n_keys=n_keys, koffs=koffs, vsel=vsel,
                          shared=kind == "gqa"),
        grid=(bsz, 2, t_q // tq),
        in_specs=[pl.BlockSpec((1, 2, tq, LANE), lambda b, p, i: (b, p, i, 0)),
                  pl.BlockSpec((1, n_keys, kblk), lambda b, p, i: (b, 0, kidx(p))),
                  pl.BlockSpec((1, vblk, n_keys // CHUNK, HEAD_DIM, CHUNK),
                               lambda b, p, i: (b, p, 0, 0, 0))],
        out_specs=pl.BlockSpec((1, tq, LANE), lambda b, p, i: (b, i, p)),
        out_shape=jax.ShapeDtypeStruct((bsz, t_q, GROUP_WIDTH), F32),
        scratch_shapes=[pltpu.VMEM((tk, 2 * tq), F32), pltpu.VMEM((tk, 2 * tq), F32),
                        pltpu.VMEM((1, 2 * tq), F32), pltpu.VMEM((1, 2 * tq), F32),
                        pltpu.VMEM((HEAD_DIM, 2 * tq), F32)],
        compiler_params=_cparams(3, VMEM_LIMIT),
        name=f"attn_dense_{kind}",
    )(q, k_all, vt_all)


def _flash(q, k_ctx, vt_ctx, k_lat, vt_lat, *, mode, kind, sink=None, bias=None):
    bsz, _, t_q, _ = q.shape
    n_ctx = k_ctx.shape[1]
    tq = TQ_NBR if mode == "nbr" else min(TQ, t_q)
    nq = t_q // tq
    t_lat = 0 if k_lat is None else k_lat.shape[1]
    if kind == "gqa":
        kblk, kidx, koffs, vblk, vsel = LANE, (lambda p: 0), (0, 0), 1, (0, 0)
    elif kind == "mha":
        kblk, kidx, koffs, vblk, vsel = LANE, (lambda p: p), (0, 0), 2, (0, 1)
    else:
        kblk, kidx, koffs, vblk, vsel = 2 * LANE, (lambda p: p), (0, LANE), 2, (0, 1)

    in_specs, args = [], []
    if sink is not None:
        in_specs.append(pl.BlockSpec(memory_space=pltpu.SMEM))
        args.append(sink)
    in_specs.append(pl.BlockSpec((1, 2, tq, LANE), lambda b, p, i: (b, p, i, 0)))
    args.append(q)

    def kv_specs(t_len):
        return [pl.BlockSpec((1, t_len, kblk), lambda b, p, i: (b, 0, kidx(p))),
                pl.BlockSpec((1, vblk, t_len // CHUNK, HEAD_DIM, CHUNK),
                             lambda b, p, i: (b, p, 0, 0, 0))]

    in_specs += kv_specs(n_ctx)
    args += [k_ctx, vt_ctx]
    if mode != "ctx":
        in_specs += kv_specs(t_lat)
        args += [k_lat, vt_lat]
    if mode == "nbr":
        def variant(i):
            return jnp.where(i == 0, 0, jnp.where(i == 1, 1,
                             jnp.where(i == nq - 2, 3, jnp.where(i == nq - 1, 4, 2))))
        in_specs.append(pl.BlockSpec((1, 2, NBR_KEYS, tq), lambda b, p, i: (variant(i), p, 0, 0)))
        args.append(bias)

    return pl.pallas_call(
        functools.partial(_flash_kernel, mode=mode, has_sink=sink is not None, tq=tq,
                          t_lat=t_lat, koffs=koffs, vsel=vsel, shared=kind == "gqa"),
        grid=(bsz, 2, nq),
        in_specs=in_specs,
        out_specs=pl.BlockSpec((1, tq, LANE), lambda b, p, i: (b, i, p)),
        out_shape=jax.ShapeDtypeStruct((bsz, t_q, GROUP_WIDTH), F32),
        compiler_params=_cparams(3, VMEM_LIMIT),
        name=f"attn_{mode}_{kind}",
    )(*args)


def _merge_kernel(h_ref, oa_ref, ob_ref, oc_ref, od_ref, g_ref, w_ref, gate_ref, o_ref):
    acc = None
    for gi, ref in enumerate((oa_ref, ob_ref, oc_ref, od_ref)):
        y = ref[0]
        y = y * lax.rsqrt(jnp.mean(y * y, axis=-1, keepdims=True) + EPS) * g_ref[gi:gi + 1, :]
        part = _dot(y.astype(BF16), w_ref[gi * GROUP_WIDTH:(gi + 1) * GROUP_WIDTH, :])
        acc = part if acc is None else acc + part
    o_ref[0] = h_ref[0] + gate_ref[0] * acc


def _merge(h, outs, gain, w_out, gate, tm):
    bsz, t_len, d = h.shape
    ospec = pl.BlockSpec((1, tm, GROUP_WIDTH), lambda b, t: (b, t, 0))
    return pl.pallas_call(
        _merge_kernel,
        grid=(bsz, t_len // tm),
        in_specs=[pl.BlockSpec((1, tm, d), lambda b, t: (b, t, 0)), ospec, ospec, ospec, ospec,
                  pl.BlockSpec((N_HEADS, GROUP_WIDTH), lambda b, t: (0, 0)),
                  pl.BlockSpec((d, d), lambda b, t: (0, 0)),
                  pl.BlockSpec((1, 1, d), lambda b, t: (b, 0, 0))],
        out_specs=pl.BlockSpec((1, tm, d), lambda b, t: (b, t, 0)),
        out_shape=jax.ShapeDtypeStruct(h.shape, F32),
        compiler_params=_cparams(2, VMEM_LIMIT),
        name="merge_out",
    )(h, *outs, gain, w_out, gate)


def _norm_mod(x, g, mod_ref):
    xn = x * lax.rsqrt(jnp.mean(x * x, axis=-1, keepdims=True) + EPS) * g
    return xn * (1.0 + mod_ref[0, 1:2, :]) + mod_ref[0, 0:1, :]


def _ffn_kernel(h_ref, mod_ref, g_ref, wg_ref, wu_ref, wd_ref, o_ref, *, n_chunks, fc):
    x = h_ref[0]
    f = _norm_mod(x, g_ref[...], mod_ref).astype(BF16)
    acc = None
    for c in range(n_chunks):
        gate = _dot(f, wg_ref[:, c * fc:(c + 1) * fc])
        up = _dot(f, wu_ref[:, c * fc:(c + 1) * fc])
        act = (gate * jax.nn.sigmoid(gate) * up).astype(BF16)
        part = _dot(act, wd_ref[c * fc:(c + 1) * fc, :])
        acc = part if acc is None else acc + part
    o_ref[0] = x + mod_ref[0, 2:3, :] * acc


def _ffn(h, mod, gain, w_gate, w_up, w_down, tm):
    bsz, t_len, d = h.shape
    d_ff = w_gate.shape[1]
    n_chunks = 2
    fc = d_ff // n_chunks
    return pl.pallas_call(
        functools.partial(_ffn_kernel, n_chunks=n_chunks, fc=fc),
        grid=(bsz, t_len // tm),
        in_specs=[pl.BlockSpec((1, tm, d), lambda b, t: (b, t, 0)),
                  pl.BlockSpec((1, 3, d), lambda b, t: (b, 0, 0)),
                  pl.BlockSpec((1, d), lambda b, t: (0, 0)),
                  pl.BlockSpec((d, d_ff), lambda b, t: (0, 0), pipeline_mode=pl.Buffered(1)),
                  pl.BlockSpec((d, d_ff), lambda b, t: (0, 0), pipeline_mode=pl.Buffered(1)),
                  pl.BlockSpec((d_ff, d), lambda b, t: (0, 0), pipeline_mode=pl.Buffered(1))],
        out_specs=pl.BlockSpec((1, tm, d), lambda b, t: (b, t, 0)),
        out_shape=jax.ShapeDtypeStruct(h.shape, F32),
        compiler_params=_cparams(2, VMEM_LIMIT),
        name="ffn_dense",
    )(h, mod, gain, w_gate, w_up, w_down)


def _route_kernel(h_ref, mod_ref, g_ref, rt_ref, f_ref, e_ref, gate_ref, rank_ref, cnt_ref,
                  carry_scr, *, tm):
    step = pl.program_id(0) * pl.num_programs(1) + pl.program_id(1)

    @pl.when(step == 0)
    def _():
        carry_scr[...] = jnp.zeros_like(carry_scr)

    f = _norm_mod(h_ref[0], g_ref[...], mod_ref)
    f_ref[0] = f
    logits = _dot_nt(rt_ref[...], f.astype(BF16))
    row = lax.broadcasted_iota(jnp.int32, (N_EXPERTS, tm), 0)
    m1 = jnp.max(logits, axis=0, keepdims=True)
    e1 = jnp.min(jnp.where(logits == m1, row, N_EXPERTS), axis=0, keepdims=True)
    rest = jnp.where(row == e1, -jnp.inf, logits)
    m2 = jnp.max(rest, axis=0, keepdims=True)
    e2 = jnp.min(jnp.where(rest == m2, row, N_EXPERTS), axis=0, keepdims=True)
    z = jnp.exp(m2 - m1)
    den = 1.0 + z
    gate_ref[0, 0:1, :] = 1.0 / den
    gate_ref[0, 1:2, :] = z / den
    e_ref[0, 0:1, :] = e1
    e_ref[0, 1:2, :] = e2

    ind = jnp.logical_or(row == e1, row == e2)
    ii = lax.broadcasted_iota(jnp.int32, (tm, tm), 0)
    jj = lax.broadcasted_iota(jnp.int32, (tm, tm), 1)
    upper = jnp.where(ii < jj, 1.0, 0.0).astype(BF16)
    before = _dot(jnp.where(ind, 1.0, 0.0).astype(BF16), upper) + carry_scr[...]
    rank_ref[0, 0:1, :] = jnp.sum(jnp.where(row == e1, before, 0.0), axis=0, keepdims=True).astype(jnp.int32)
    rank_ref[0, 1:2, :] = jnp.sum(jnp.where(row == e2, before, 0.0), axis=0, keepdims=True).astype(jnp.int32)
    carry_scr[...] = carry_scr[...] + jnp.sum(jnp.where(ind, 1.0, 0.0), axis=1, keepdims=True)
    cnt_ref[...] = carry_scr[...].astype(jnp.int32)


def _route(h, mod, gain, router_t, tm):
    bsz, t_len, d = h.shape
    nt = t_len // tm
    n_tiles = bsz * nt
    small = lambda: pl.BlockSpec((1, 2, tm), lambda b, t: (b * nt + t, 0, 0))
    return pl.pallas_call(
        functools.partial(_route_kernel, tm=tm),
        grid=(bsz, nt),
        in_specs=[pl.BlockSpec((1, tm, d), lambda b, t: (b, t, 0)),
                  pl.BlockSpec((1, 3, d), lambda b, t: (b, 0, 0)),
                  pl.BlockSpec((1, d), lambda b, t: (0, 0)),
                  pl.BlockSpec((N_EXPERTS, d), lambda b, t: (0, 0))],
        out_specs=[pl.BlockSpec((1, tm, d), lambda b, t: (b, t, 0)),
                   small(), small(), small(),
                   pl.BlockSpec((N_EXPERTS, 1), lambda b, t: (0, 0))],
        out_shape=[jax.ShapeDtypeStruct(h.shape, F32),
                   jax.ShapeDtypeStruct((n_tiles, 2, tm), jnp.int32),
                   jax.ShapeDtypeStruct((n_tiles, 2, tm), F32),
                   jax.ShapeDtypeStruct((n_tiles, 2, tm), jnp.int32),
                   jax.ShapeDtypeStruct((N_EXPERTS, 1), jnp.int32)],
        scratch_shapes=[pltpu.VMEM((N_EXPERTS, 1), F32)],
        compiler_params=_cparams(2, VMEM_LIMIT),
        name="moe_route",
    )(h, mod, gain, router_t)


def _dispatch_kernel(pstart_ref, e_ref, rank_ref, f_ref, xs_in_ref, xs_ref, sem, *, tm):
    del xs_in_ref

    def row_copy(i, k):
        dest = pstart_ref[e_ref[0, k, i]] + rank_ref[0, k, i]
        return pltpu.make_async_copy(f_ref.at[pl.ds(i, 1)], xs_ref.at[pl.ds(dest, 1)], sem)

    def issue(i, carry):
        row_copy(i, 0).start()
        row_copy(i, 1).start()
        return carry
    lax.fori_loop(0, tm, issue, 0)
    for _ in range(2):
        pltpu.make_async_copy(f_ref, xs_ref.at[pl.ds(0, tm)], sem).wait()


def _dispatch(f_flat, e_idx, rank, pstart, n_rows, tm):
    n_tok, d = f_flat.shape
    xs0 = jnp.zeros((n_rows, d), F32)
    smem = lambda: pl.BlockSpec((1, 2, tm), lambda t, ps: (t, 0, 0), memory_space=pltpu.SMEM)
    return pl.pallas_call(
        functools.partial(_dispatch_kernel, tm=tm),
        grid_spec=pltpu.PrefetchScalarGridSpec(
            num_scalar_prefetch=1,
            grid=(n_tok // tm,),
            in_specs=[smem(), smem(),
                      pl.BlockSpec((tm, d), lambda t, ps: (t, 0)),
                      pl.BlockSpec(memory_space=pl.ANY)],
            out_specs=pl.BlockSpec(memory_space=pl.ANY),
            scratch_shapes=[pltpu.SemaphoreType.DMA(())]),
        out_shape=jax.ShapeDtypeStruct((n_rows, d), F32),
        input_output_aliases={4: 0},
        compiler_params=pltpu.CompilerParams(dimension_semantics=("arbitrary",),
                                             has_side_effects=True),
        name="moe_dispatch",
    )(pstart, e_idx, rank, f_flat, xs0)


def _moe_kernel(be_ref, nrow_ref, nused_ref, x_ref, wg_ref, wu_ref, wd_ref, y_ref, *, tb, sub):
    del be_ref, nused_ref
    blk = pl.program_id(0)
    fi = pl.program_id(1)
    n_valid = nrow_ref[blk]

    def compute(rows):
        x = x_ref[0:rows, :].astype(BF16)
        gate = _dot(x, wg_ref[0].astype(BF16))
        up = _dot(x, wu_ref[0].astype(BF16))
        act = (gate * jax.nn.sigmoid(gate) * up).astype(BF16)
        part = _dot(act, wd_ref[0].astype(BF16))

        @pl.when(fi == 0)
        def _():
            y_ref[0:rows, :] = part
            if rows < tb:
                y_ref[rows:tb, :] = jnp.zeros((tb - rows, y_ref.shape[1]), F32)

        @pl.when(fi != 0)
        def _():
            y_ref[0:rows, :] += part

    @pl.when(n_valid > sub)
    def _():
        compute(tb)

    @pl.when(jnp.logical_and(n_valid > 0, n_valid <= sub))
    def _():
        compute(sub)

    @pl.when(jnp.logical_and(n_valid == 0, fi == 0))
    def _():
        y_ref[...] = jnp.zeros(y_ref.shape, F32)


def _moe_blocks(xs, block_e, block_rows, n_used, w_gate, w_up, w_down):
    n_rows, d = xs.shape
    nb = n_rows // MOE_TB
    d_ff = w_gate.shape[2]
    nf = d_ff // MOE_TF

    def bidx(b, be, nr, nu):
        return jnp.minimum(b, nu[0] - 1)

    def fidx(b, f, nu):
        return jnp.where(b < nu[0], f, nf - 1)

    return pl.pallas_call(
        functools.partial(_moe_kernel, tb=MOE_TB, sub=MOE_TB // 2),
        grid_spec=pltpu.PrefetchScalarGridSpec(
            num_scalar_prefetch=3,
            grid=(nb, nf),
            in_specs=[pl.BlockSpec((MOE_TB, d), lambda b, f, be, nr, nu: (bidx(b, be, nr, nu), 0)),
                      pl.BlockSpec((1, d, MOE_TF),
                                   lambda b, f, be, nr, nu: (be[bidx(b, be, nr, nu)], 0, fidx(b, f, nu))),
                      pl.BlockSpec((1, d, MOE_TF),
                                   lambda b, f, be, nr, nu: (be[bidx(b, be, nr, nu)], 0, fidx(b, f, nu))),
                      pl.BlockSpec((1, MOE_TF, d),
                                   lambda b, f, be, nr, nu: (be[bidx(b, be, nr, nu)], fidx(b, f, nu), 0))],
            out_specs=pl.BlockSpec((MOE_TB, d), lambda b, f, be, nr, nu: (b, 0))),
        out_shape=jax.ShapeDtypeStruct((n_rows, d), F32),
        compiler_params=_cparams(2, VMEM_LIMIT),
        name="moe_experts",
    )(block_e, block_rows, n_used, xs, w_gate, w_up, w_down)


def _combine_kernel(pstart_ref, e_ref, rank_ref, h_ref, g2_ref, gates_ref, y_ref, o_ref,
                    buf0, buf1, sem, *, tm):
    def row_copy(i, k, buf):
        src = pstart_ref[e_ref[0, k, i]] + rank_ref[0, k, i]
        return pltpu.make_async_copy(y_ref.at[pl.ds(src, 1)], buf.at[pl.ds(i, 1)], sem)

    def issue(i, carry):
        row_copy(i, 0, buf0).start()
        row_copy(i, 1, buf1).start()
        return carry
    lax.fori_loop(0, tm, issue, 0)
    pltpu.make_async_copy(y_ref.at[pl.ds(0, tm)], buf0, sem).wait()
    pltpu.make_async_copy(y_ref.at[pl.ds(0, tm)], buf1, sem).wait()

    gates = gates_ref[...]
    moe = buf0[...] * gates[:, 0:1] + buf1[...] * gates[:, 1:2]
    o_ref[0] = h_ref[0] + g2_ref[0] * moe


def _combine(h, gate2, gates_col, y, e_idx, rank, pstart, tm):
    bsz, t_len, d = h.shape
    nt = t_len // tm
    smem = lambda: pl.BlockSpec((1, 2, tm), lambda b, t, ps: (b * nt + t, 0, 0),
                                memory_space=pltpu.SMEM)
    return pl.pallas_call(
        functools.partial(_combine_kernel, tm=tm),
        grid_spec=pltpu.PrefetchScalarGridSpec(
            num_scalar_prefetch=1,
            grid=(bsz, nt),
            in_specs=[smem(), smem(),
                      pl.BlockSpec((1, tm, d), lambda b, t, ps: (b, t, 0)),
                      pl.BlockSpec((1, 1, d), lambda b, t, ps: (b, 0, 0)),
                      pl.BlockSpec((tm, 2), lambda b, t, ps: (b * nt + t, 0)),
                      pl.BlockSpec(memory_space=pl.ANY)],
            out_specs=pl.BlockSpec((1, tm, d), lambda b, t, ps: (b, t, 0)),
            scratch_shapes=[pltpu.VMEM((tm, d), F32), pltpu.VMEM((tm, d), F32),
                            pltpu.SemaphoreType.DMA(())]),
        out_shape=jax.ShapeDtypeStruct(h.shape, F32),
        compiler_params=_cparams(2, VMEM_LIMIT),
        name="moe_combine",
    )(pstart, e_idx, rank, h, gate2, gates_col, y)


def _moe(h, mod, gain, router, w_gate, w_up, w_down):
    bsz, t_len, d = h.shape
    n_tok = bsz * t_len
    tm = min(TM_LAT, t_len)
    f, e_idx, gates, rank, counts = _route(h, mod, gain, router.T.astype(BF16), tm)

    counts = counts[:, 0]
    padded = (counts + MOE_TB - 1) // MOE_TB * MOE_TB
    pends = jnp.cumsum(padded)
    pstart = (pends - padded).astype(jnp.int32)
    nb = (2 * n_tok + N_EXPERTS * (MOE_TB - 1) + MOE_TB - 1) // MOE_TB
    bstart = jnp.arange(nb, dtype=jnp.int32) * MOE_TB
    block_e = jnp.minimum(jnp.sum(bstart[:, None] >= pends[None, :], axis=1), N_EXPERTS - 1).astype(jnp.int32)
    block_rows = jnp.clip(pstart[block_e] + counts[block_e] - bstart, 0, MOE_TB).astype(jnp.int32)
    n_used = (pends[-1] // MOE_TB).astype(jnp.int32).reshape(1)

    xs = _dispatch(f.reshape(n_tok, d), e_idx, rank, pstart, nb * MOE_TB, tm)
    y = _moe_blocks(xs, block_e, block_rows, n_used, w_gate, w_up, w_down)
    gates_col = gates.transpose(0, 2, 1).reshape(n_tok, 2)
    return _combine(h, mod[:, 2:3, :], gates_col, y, e_idx, rank, pstart, tm)


def _rope_tables(n_tokens, rot_dim):
    t = jnp.arange(n_tokens)
    row = (t // GRID_W).astype(F32)[:, None]
    col = (t % GRID_W).astype(F32)[:, None]
    half = rot_dim // 2
    inv_freq = ROPE_THETA ** (-jnp.arange(0, half, 2, dtype=F32) / half)
    ang = jnp.concatenate([row * inv_freq, row * inv_freq, col * inv_freq, col * inv_freq], axis=-1)
    return jnp.cos(ang), jnp.sin(ang)


def _rope_inputs(n_tokens):
    cos64, sin64 = _rope_tables(n_tokens, HEAD_DIM)
    lane = np.arange(HEAD_DIM)
    upper = jnp.asarray((lane % 32) >= 16)
    cos = jnp.tile(cos64, (1, 2))
    sin_a = jnp.tile(jnp.where(upper, sin64, 0.0), (1, 2))
    sin_b = jnp.tile(jnp.where(upper, 0.0, -sin64), (1, 2))

    cos32, sin32 = _rope_tables(n_tokens, MLA_ROPE)
    lane = np.arange(MLA_ROPE)
    upper = jnp.asarray((lane % 16) >= 8)
    ones = jnp.ones((n_tokens, MLA_NOPE), F32)
    zeros = jnp.zeros((n_tokens, MLA_NOPE), F32)
    pad1 = jnp.ones((n_tokens, LANE - MLA_QK), F32)
    pad0 = jnp.zeros((n_tokens, LANE - MLA_QK), F32)
    cos_m = jnp.concatenate([ones, cos32, pad1], axis=1)
    sin_am = jnp.concatenate([zeros, jnp.where(upper, sin32, 0.0), pad0], axis=1)
    sin_bm = jnp.concatenate([zeros, jnp.where(upper, 0.0, -sin32), pad0], axis=1)
    return cos, sin_a, sin_b, cos_m, sin_am, sin_bm


def _block_diag_ones(width, block):
    idx = np.arange(width) // block
    return jnp.asarray(idx[:, None] == idx[None, :], dtype=BF16)


def _layer_weights(l, norm1_g, w_in, a_qn, a_kn, b_qn, b_kn, c_qn, c_kn, d_q_norm, d_w_uq,
                   d_kv_norm, d_w_ukv, d_qn, d_kn):
    w = w_in[l]
    d = w.shape[0]

    def cols(a, b):
        return w[:, a:b]

    def perm_q(a):
        q = cols(a, a + 256).reshape(d, 4, HEAD_DIM)
        return q[:, jnp.array([0, 2, 1, 3]), :].reshape(d, 256)

    z = lambda n: jnp.zeros((d, n), w.dtype)
    w_new = jnp.concatenate([
        perm_q(0), perm_q(512), cols(1024, 1280), cols(1280, 1536),
        cols(256, 384), cols(768, 896),
        cols(384, 512), cols(896, 1024), cols(1536, 1792),
        cols(1792, 1984), z(64),
        cols(1984, 2112),
        z(MLA_NOPE), cols(2112, 2144), z(LANE - MLA_QK),
    ], axis=1).astype(BF16)

    t4 = lambda g: jnp.tile(g[l], 4)
    gq = jnp.stack([t4(a_qn), t4(b_qn), t4(c_qn)])
    gk = jnp.stack([t4(c_kn), jnp.concatenate([jnp.tile(a_kn[l], 2), jnp.tile(b_kn[l], 2)])])
    pad96 = lambda g: jnp.tile(jnp.concatenate([g[l], jnp.zeros((LANE - MLA_QK,), F32)]), 2)[None, :]
    gdq = jnp.concatenate([d_q_norm[l], jnp.zeros((256 - MLA_Q_RANK,), F32)])[None, :]

    wuq = d_w_uq[l].reshape(MLA_Q_RANK, N_HEADS, MLA_QK)
    wuq = jnp.pad(wuq, ((0, 256 - MLA_Q_RANK), (0, 0), (0, LANE - MLA_QK))).reshape(256, N_HEADS * LANE)
    wukv = d_w_ukv[l].reshape(MLA_KV_RANK, N_HEADS, MLA_NOPE + HEAD_DIM)
    wk = jnp.pad(wukv[:, :, :MLA_NOPE], ((0, 0), (0, 0), (0, LANE - MLA_NOPE))).reshape(MLA_KV_RANK, N_HEADS * LANE)
    wv = wukv[:, :, MLA_NOPE:].reshape(MLA_KV_RANK, N_HEADS * HEAD_DIM)
    return dict(
        g1=norm1_g[l][None, :], w_in=w_new, gq=gq, gk=gk, gdq=gdq, gdkv=d_kv_norm[l][None, :],
        gdqn=pad96(d_qn), gdkn=pad96(d_kn),
        w_uq=wuq.astype(BF16), w_ukv=jnp.concatenate([wk, wv], axis=1).astype(BF16),
        m64=_block_diag_ones(256, HEAD_DIM), m128=_block_diag_ones(256, LANE))


N_DR = 2 * NA_KH - 1
N_DC = 2 * NA_KW - 1


def _toeplitz_kernel(rpb_ref, onehot_ref, o_ref):
    r = rpb_ref[...]
    hi = r.astype(BF16)
    r1 = r - hi.astype(F32)
    mid = r1.astype(BF16)
    lo = (r1 - mid.astype(F32)).astype(BF16)
    e = onehot_ref[...]
    o_ref[...] = _dot(hi, e) + _dot(mid, e) + _dot(lo, e)


def _nbr_bias(rpb, rows):
    n_heads = rpb.shape[0]
    m = np.arange(32)[:, None, None]
    kc = np.arange(GRID_W)[None, :, None]
    qc = np.arange(GRID_W)[None, None, :]
    onehot = jnp.asarray((m == kc - qc + NA_KW - 1).reshape(32, GRID_W * GRID_W), dtype=BF16)
    rpb_p = jnp.pad(rpb.reshape(n_heads * N_DR, N_DC), ((0, 64 - n_heads * N_DR), (0, 32 - N_DC)))
    toe = pl.pallas_call(
        _toeplitz_kernel,
        out_shape=jax.ShapeDtypeStruct((64, GRID_W * GRID_W), F32),
        name="nbr_bias_toeplitz",
    )(rpb_p, onehot)
    toe = toe[:n_heads * N_DR].reshape(n_heads, N_DR, GRID_W, GRID_W)

    n_krow = NBR_KEYS // GRID_W
    toe_p = jnp.pad(toe, ((0, 0), (n_krow, n_krow), (0, 0), (0, 0)))
    cs = np.clip(np.arange(GRID_W) - NA_KW // 2, 0, GRID_W - NA_KW)
    kcol = np.arange(GRID_W)[:, None]
    col_ok = (kcol >= cs[None, :]) & (kcol < cs[None, :] + NA_KW)
    variants = []
    for r in (0, 2, 4, rows - 4, rows - 2):
        start_row = min(max(r - NA_KH // 2, 0), rows - n_krow)
        per_q = []
        for q in range(2):
            qr = r + q
            rs = min(max(qr - NA_KH // 2, 0), rows - NA_KH)
            lo = start_row - qr + NA_KH - 1 + n_krow
            slab = toe_p[:, lo:lo + n_krow]
            kr = start_row + np.arange(n_krow)
            row_ok = (kr >= rs) & (kr < rs + NA_KH)
            ok = row_ok[:, None, None] & col_ok[None, :, :]
            per_q.append(jnp.where(jnp.asarray(ok)[None], slab * LOG2E, NEG))
        variants.append(jnp.stack(per_q, axis=3))
    return jnp.stack(variants, axis=0).reshape(5, n_heads, NBR_KEYS, TQ_NBR)


def _mixers(h_lat, h_ctx, mod_lat, mod_ctx, lw, rope_tabs, a_sink, nbr_bias, with_ctx_out):
    lat = _proj(h_lat, mod_lat, lw, rope_tabs, TM_LAT)
    cx = _proj(h_ctx, mod_ctx, lw, None, h_ctx.shape[1])

    def attend(q, name, kind, mode, **kw):
        if mode == "ctx":
            return _flash(q[name + "_q"], cx[name + "_k"], cx[name + "_vt"], None, None,
                          mode=mode, kind=kind, **kw)
        return _flash(q[name + "_q"], cx[name + "_k"], cx[name + "_vt"], lat[name + "_k"],
                      lat[name + "_vt"], mode=mode, kind=kind, **kw)

    def dense(name, kind):
        k_all = jnp.concatenate([lat[name + "_k"], cx[name + "_k"]], axis=1)
        vt_all = jnp.concatenate([lat[name + "_vt"], cx[name + "_vt"]], axis=2)
        return _flash_dense(lat[name + "_q"], k_all, vt_all, kind)

    outs_lat = (attend(lat, "a", "gqa", "window", sink=a_sink),
                dense("b", "gqa"),
                attend(lat, "c", "mha", "nbr", bias=nbr_bias),
                dense("d", "mla"))
    outs_ctx = None
    if with_ctx_out:
        outs_ctx = (attend(cx, "a", "gqa", "ctx", sink=a_sink),
                    attend(cx, "b", "gqa", "ctx"),
                    attend(cx, "c", "mha", "ctx"),
                    attend(cx, "d", "mla", "ctx"))
    return outs_lat, outs_ctx


def kernel(x, c, ctx, c_ctx, ada_w, ada_b, norm1_g, norm2_g, w_in, a_qn, a_kn, a_sink, b_qn, b_kn,
           c_qn, c_kn, c_rpb, d_q_norm, d_w_uq, d_kv_norm, d_w_ukv, d_qn, d_kn, out_norm_g, w_out,
           ffn_w_gate, ffn_w_up, ffn_w_down, moe_router, moe_w_gate, moe_w_up, moe_w_down):
    bsz, seq, d = x.shape
    depth = ada_w.shape[0]
    n_ctx = ctx.shape[1]
    assert d == D_MODEL and seq % TM_LAT == 0 and seq >= NBR_KEYS and n_ctx % TQ == 0
    assert (seq // GRID_W) % 2 == 0 and seq // TQ_NBR >= 5

    cvec = jnp.concatenate([c, c_ctx[None, :], jnp.zeros((8 - bsz - 1, d), F32)], axis=0)
    mod_all = _adaln(cvec, ada_w, ada_b)
    rope_tabs = _rope_inputs(seq)

    h_lat, h_ctx = x, ctx
    for layer in range(depth):
        last = layer == depth - 1
        mod = mod_all[layer].reshape(8, 6, d)
        mod_lat = mod[:bsz]
        mod_ctx = jnp.broadcast_to(mod[bsz:bsz + 1], (bsz, 6, d))
        lw = _layer_weights(layer, norm1_g, w_in, a_qn, a_kn, b_qn, b_kn, c_qn, c_kn, d_q_norm,
                            d_w_uq, d_kv_norm, d_w_ukv, d_qn, d_kn)
        nbr_bias = _nbr_bias(c_rpb[layer], seq // GRID_W)
        outs_lat, outs_ctx = _mixers(h_lat, h_ctx, mod_lat[:, 0:2], mod_ctx[:, 0:2], lw, rope_tabs,
                                     a_sink[layer], nbr_bias, not last)
        gain_o = out_norm_g[layer].reshape(N_HEADS, GROUP_WIDTH)
        w_o = w_out[layer].astype(BF16)
        h_lat = _merge(h_lat, outs_lat, gain_o, w_o, mod_lat[:, 2:3], TM_LAT)
        if not last:
            h_ctx = _merge(h_ctx, outs_ctx, gain_o, w_o, mod_ctx[:, 2:3], n_ctx)

        j = layer // 2
        g2 = norm2_g[layer][None, :]
        if layer % 2 == 0:
            wg, wu, wd = (ffn_w_gate[j].astype(BF16), ffn_w_up[j].astype(BF16),
                          ffn_w_down[j].astype(BF16))
            h_lat = _ffn(h_lat, mod_lat[:, 3:6], g2, wg, wu, wd, TM_LAT)
            if not last:
                h_ctx = _ffn(h_ctx, mod_ctx[:, 3:6], g2, wg, wu, wd, n_ctx)
        else:
            h_lat = _moe(h_lat, mod_lat[:, 3:6], g2, moe_router[j], moe_w_gate[j], moe_w_up[j],
                         moe_w_down[j])
            if not last:
                h_ctx = _moe(h_ctx, mod_ctx[:, 3:6], g2, moe_router[j], moe_w_gate[j], moe_w_up[j],
                             moe_w_down[j])
    return h_lat
```
